```python
import math
import jax, jax.numpy as jnp
from jax import lax
import numpy as np

D_MODEL = 2048
BATCH = 2
SEQ = 8192
DEPTH = 2

HEAD_DIM = 128
N_HEADS = D_MODEL // HEAD_DIM
N_FOX_HEADS = N_HEADS // 2
N_NSA_HEADS = N_HEADS - N_FOX_HEADS
NSA_GROUP = 4
NSA_KV_HEADS = N_NSA_HEADS // NSA_GROUP
D_MIX = (N_FOX_HEADS + N_NSA_HEADS) * HEAD_DIM
CMP_BLOCK = 32
CMP_STRIDE = 16
SLC_BLOCK = 64
SLC_TOPK = 16
WINDOW = 512
Q_BLOCK = 128
D_FF = -(-8 * D_MODEL // (3 * 256)) * 256
RMS_EPS = 1e-6
NEG_INF = -1e30
FORCE_SCORE = 1e9

FOX_W = N_FOX_HEADS * HEAD_DIM
NSA_W = N_NSA_HEADS * HEAD_DIM
KV_W = NSA_KV_HEADS * HEAD_DIM
IN_SPLITS = (FOX_W, FOX_W, FOX_W, N_FOX_HEADS, NSA_W, KV_W, KV_W, KV_W, KV_W, KV_W, KV_W, 3 * N_NSA_HEADS)
IN_WIDTH = sum(IN_SPLITS)

kernel_name = "hybrid_fox_nsa_parallel_heads"


def _rms_norm(x, g):
    xf = x.astype(jnp.float32)
    y = xf * lax.rsqrt(jnp.mean(xf * xf, axis=-1, keepdims=True) + RMS_EPS)
    return (y * g.astype(jnp.float32)).astype(x.dtype)


def _masked_softmax(s, mask):
    s = jnp.where(mask, s, NEG_INF)
    m = jnp.max(s, axis=-1, keepdims=True)
    p = jnp.where(mask, jnp.exp(s - m), 0.0)
    return p / jnp.maximum(jnp.sum(p, axis=-1, keepdims=True), 1e-30)


def _alibi_slopes(n):
    return jnp.exp2(-8.0 * jnp.arange(1, n + 1, dtype=jnp.float32) / n)


def _overlap_matrix(n_cmp, n_slc):
    ci = np.arange(n_cmp)[:, None]
    sj = np.arange(n_slc)[None, :]
    ov = (ci * CMP_STRIDE < (sj + 1) * SLC_BLOCK) & (ci * CMP_STRIDE + CMP_BLOCK > sj * SLC_BLOCK)
    return ov.astype(np.float32)


def _compress(tok, pos_emb, w):
    B, T, G, Dh = tok.shape
    chunks = tok.reshape(B, T // CMP_STRIDE, CMP_STRIDE, G, Dh)
    blocks = jnp.concatenate([chunks[:, :-1], chunks[:, 1:]], axis=2)
    return jnp.einsum("bnlgd,lde->bnge", blocks + pos_emb[None, None, :, None, :], w)


def _hybrid_mixer(h, w_in, fox_forget_bias, fox_q_norm, fox_k_norm, nsa_q_norm, cmp_k_norm,
                  slc_k_norm, win_k_norm, cmp_pos_k, cmp_pos_v, cmp_w_k, cmp_w_v, w_out):
    B, T, _ = h.shape
    dt = h.dtype
    f32 = jnp.float32
    Hf, Hn, G, R, Dh = N_FOX_HEADS, N_NSA_HEADS, NSA_KV_HEADS, NSA_GROUP, HEAD_DIM
    scale = HEAD_DIM ** -0.5
    n_cmp = T // CMP_STRIDE - 1
    n_slc = T // SLC_BLOCK
    k_sel = min(SLC_TOPK, n_slc)
    n_qb = T // Q_BLOCK

    proj = h @ w_in
    cuts = [int(c) for c in np.cumsum(IN_SPLITS)[:-1]]
    qf, kf, vf, ff, qn, kc, vc, ks, vs, kw, vw, gt = jnp.split(proj, cuts, axis=-1)

    qf = _rms_norm(qf.reshape(B, T, Hf, Dh), fox_q_norm)
    kf = _rms_norm(kf.reshape(B, T, Hf, Dh), fox_k_norm)
    vf = vf.reshape(B, T, Hf, Dh)
    log_f = jax.nn.log_sigmoid(ff.astype(f32) + fox_forget_bias.astype(f32))
    cum_f = jnp.transpose(jnp.cumsum(log_f, axis=1), (0, 2, 1))

    qn = _rms_norm(qn.reshape(B, T, G, R, Dh), nsa_q_norm)
    k_cmp = _rms_norm(_compress(kc.reshape(B, T, G, Dh), cmp_pos_k, cmp_w_k), cmp_k_norm)
    v_cmp = _compress(vc.reshape(B, T, G, Dh), cmp_pos_v, cmp_w_v)
    cmp_end = jnp.arange(n_cmp) * CMP_STRIDE + (CMP_BLOCK - 1)
    k_slc = jnp.transpose(_rms_norm(ks.reshape(B, n_slc, SLC_BLOCK, G, Dh), slc_k_norm), (0, 3, 1, 2, 4))
    v_slc = jnp.transpose(vs.reshape(B, n_slc, SLC_BLOCK, G, Dh), (0, 3, 1, 2, 4))
    pad = ((0, 0), (WINDOW, 0), (0, 0), (0, 0))
    k_win = jnp.pad(_rms_norm(kw.reshape(B, T, G, Dh), win_k_norm), pad)
    v_win = jnp.pad(vw.reshape(B, T, G, Dh), pad)
    gates = jax.nn.sigmoid(gt.astype(f32)).astype(dt).reshape(B, T, G, R, 3)

    slopes = _alibi_slopes(Hn).reshape(G, R)
    overlap = jnp.asarray(_overlap_matrix(n_cmp, n_slc))
    key_pos = jnp.arange(T)
    blk_ids = jnp.arange(n_slc)
    bi = jnp.arange(B)[:, None, None, None]
    gi = jnp.arange(G)[None, :, None, None]
    win_off = jnp.arange(Q_BLOCK + WINDOW) - WINDOW
    slc_off = jnp.arange(SLC_BLOCK)

    def query_block(i):
        q0 = i * Q_BLOCK
        tpos = q0 + jnp.arange(Q_BLOCK)

        qf_b = lax.dynamic_slice_in_dim(qf, q0, Q_BLOCK, axis=1)
        cq = lax.dynamic_slice_in_dim(cum_f, q0, Q_BLOCK, axis=2)
        s = jnp.einsum("bqhd,bkhd->bhqk", qf_b, kf).astype(f32) * scale
        s = s + cq[..., None] - cum_f[:, :, None, :]
        p = _masked_softmax(s, key_pos[None, :] <= tpos[:, None])
        o_fox = jnp.einsum("bhqk,bkhd->bqhd", p.astype(dt), vf)

        qn_b = lax.dynamic_slice_in_dim(qn, q0, Q_BLOCK, axis=1)

        dist_c = (tpos[:, None] - cmp_end[None, :]).astype(f32)
        s_c = jnp.einsum("bqgrd,bngd->bgrqn", qn_b, k_cmp).astype(f32) * scale
        s_c = s_c - slopes[None, :, :, None, None] * dist_c
        p_c = _masked_softmax(s_c, dist_c >= 0)
        o_c = jnp.einsum("bgrqn,bngd->bqgrd", p_c.astype(dt), v_cmp)

        imp = jnp.einsum("bgrqn,nj->bgqj", p_c, overlap)
        cur = tpos // SLC_BLOCK
        forced = (blk_ids[None, :] == 0) | (blk_ids[None, :] == cur[:, None]) | (blk_ids[None, :] == cur[:, None] - 1)
        valid = blk_ids[None, :] * SLC_BLOCK <= tpos[:, None]
        imp = jnp.where(forced, FORCE_SCORE, jnp.where(valid, imp, -FORCE_SCORE))
        _, idx = lax.top_k(imp, k_sel)
        k_g = k_slc[bi, gi, idx]
        v_g = v_slc[bi, gi, idx]
        spos = idx[..., None] * SLC_BLOCK + slc_off
        dist_s = (tpos[None, None, :, None, None] - spos).astype(f32)[:, :, None]
        s_s = jnp.einsum("bqgrd,bgqkld->bgrqkl", qn_b, k_g).astype(f32) * scale
        s_s = s_s - slopes[None, :, :, None, None, None] * dist_s
        flat = (B, G, R, Q_BLOCK, k_sel * SLC_BLOCK)
        mask_s = jnp.broadcast_to(dist_s >= 0, s_s.shape).reshape(flat)
        p_s = _masked_softmax(s_s.reshape(flat), mask_s).reshape(s_s.shape)
        o_s = jnp.einsum("bgrqkl,bgqkld->bqgrd", p_s.astype(dt), v_g)

        k_wb = lax.dynamic_slice_in_dim(k_win, q0, Q_BLOCK + WINDOW, axis=1)
        v_wb = lax.dynamic_slice_in_dim(v_win, q0, Q_BLOCK + WINDOW, axis=1)
        wpos = q0 + win_off
        dist_w = tpos[:, None] - wpos[None, :]
        mask_w = (dist_w >= 0) & (dist_w < WINDOW) & (wpos[None, :] >= 0)
        s_w = jnp.einsum("bqgrd,bkgd->bgrqk", qn_b, k_wb).astype(f32) * scale
        s_w = s_w - slopes[None, :, :, None, None] * dist_w.astype(f32)
        p_w = _masked_softmax(s_w, mask_w)
        o_w = jnp.einsum("bgrqk,bkgd->bqgrd", p_w.astype(dt), v_wb)

        g_b = lax.dynamic_slice_in_dim(gates, q0, Q_BLOCK, axis=1)
        o_nsa = g_b[..., 0:1] * o_c + g_b[..., 1:2] * o_s + g_b[..., 2:3] * o_w
        return o_fox, o_nsa.reshape(B, Q_BLOCK, Hn, Dh)

    o_fox, o_nsa = lax.map(query_block, jnp.arange(n_qb))
    o_fox = jnp.moveaxis(o_fox, 0, 1).reshape(B, T, FOX_W)
    o_nsa = jnp.moveaxis(o_nsa, 0, 1).reshape(B, T, NSA_W)
    return jnp.concatenate([o_fox, o_nsa], axis=-1) @ w_out


def _swiglu(h, w_gate, w_up, w_down):
    return (jax.nn.silu(h @ w_gate) * (h @ w_up)) @ w_down


def setup_inputs(seed: int = 0) -> dict:
    key = jax.random.key(seed)
    ks = jax.random.split(key, 20)
    f32 = jnp.float32
    nrm = lambda k, shape, s: jax.random.normal(k, shape, f32) * s
    gain = lambda k, shape: 1.0 + 0.02 * jax.random.normal(k, shape, f32)
    L = DEPTH
    return {
        "x": nrm(ks[0], (BATCH, SEQ, D_MODEL), 1.0),
        "attn_norm": gain(ks[1], (L, D_MODEL)),
        "w_in": nrm(ks[2], (L, D_MODEL, IN_WIDTH), D_MODEL ** -0.5),
        "fox_forget_bias": 3.0 + 0.5 * jax.random.normal(ks[3], (L, N_FOX_HEADS), f32),
        "fox_q_norm": gain(ks[4], (L, HEAD_DIM)),
        "fox_k_norm": gain(ks[5], (L, HEAD_DIM)),
        "nsa_q_norm": gain(ks[6], (L, HEAD_DIM)),
        "cmp_k_norm": gain(ks[7], (L, HEAD_DIM)),
        "slc_k_norm": gain(ks[8], (L, HEAD_DIM)),
        "win_k_norm": gain(ks[9], (L, HEAD_DIM)),
        "cmp_pos_k": nrm(ks[10], (L, CMP_BLOCK, HEAD_DIM), 0.02),
        "cmp_pos_v": nrm(ks[11], (L, CMP_BLOCK, HEAD_DIM), 0.02),
        "cmp_w_k": nrm(ks[12], (L, CMP_BLOCK, HEAD_DIM, HEAD_DIM), (CMP_BLOCK * HEAD_DIM) ** -0.5),
        "cmp_w_v": nrm(ks[13], (L, CMP_BLOCK, HEAD_DIM, HEAD_DIM), (CMP_BLOCK * HEAD_DIM) ** -0.5),
        "w_out": nrm(ks[14], (L, D_MIX, D_MODEL), D_MIX ** -0.5),
        "ffn_norm": gain(ks[15], (L, D_MODEL)),
        "w_gate": nrm(ks[16], (L, D_MODEL, D_FF), D_MODEL ** -0.5),
        "w_up": nrm(ks[17], (L, D_MODEL, D_FF), D_MODEL ** -0.5),
        "w_down": nrm(ks[18], (L, D_FF, D_MODEL), D_FF ** -0.5),
    }


def reference(x, attn_norm, w_in, fox_forget_bias, fox_q_norm, fox_k_norm, nsa_q_norm, cmp_k_norm,
              slc_k_norm, win_k_norm, cmp_pos_k, cmp_pos_v, cmp_w_k, cmp_w_v, w_out, ffn_norm,
              w_gate, w_up, w_down):
    for l in range(DEPTH):
        h = _rms_norm(x, attn_norm[l])
        x = x + _hybrid_mixer(h, w_in[l], fox_forget_bias[l], fox_q_norm[l], fox_k_norm[l], nsa_q_norm[l],
                              cmp_k_norm[l], slc_k_norm[l], win_k_norm[l], cmp_pos_k[l], cmp_pos_v[l],
                              cmp_w_k[l], cmp_w_v[l], w_out[l])
        h = _rms_norm(x, ffn_norm[l])
        x = x + _swiglu(h, w_gate[l], w_up[l], w_down[l])
    return x
```

```python
import functools

import jax
import jax.numpy as jnp
from jax import lax
from jax.experimental import pallas as pl
from jax.experimental.pallas import tpu as pltpu

HEAD_DIM = 128
N_FOX_HEADS = 8
N_NSA_HEADS = 8
NSA_GROUP = 4
NSA_KV_HEADS = N_NSA_HEADS // NSA_GROUP
CMP_BLOCK = 32
CMP_STRIDE = 16
SLC_BLOCK = 64
SLC_TOPK = 16
WINDOW = 512
RMS_EPS = 1e-6
NEG_INF = -1e30
FORCE_SCORE = 1e9

FOX_W = N_FOX_HEADS * HEAD_DIM
NSA_W = N_NSA_HEADS * HEAD_DIM
KV_W = NSA_KV_HEADS * HEAD_DIM
LANES = 128
VMEM_LIMIT = 56 * 1024 * 1024

F32 = jnp.float32
BF16 = jnp.bfloat16

COL_QF = 0
COL_KF = COL_QF + N_FOX_HEADS
COL_VF = COL_KF + N_FOX_HEADS
COL_QN = COL_VF + N_FOX_HEADS
COL_KC = COL_QN + N_NSA_HEADS
COL_VC = COL_KC + NSA_KV_HEADS
COL_KS = COL_VC + NSA_KV_HEADS
COL_VS = COL_KS + NSA_KV_HEADS
COL_KW = COL_VS + NSA_KV_HEADS
COL_VW = COL_KW + NSA_KV_HEADS
MAIN_W = (COL_VW + NSA_KV_HEADS) * HEAD_DIM


def _params(*sem):
    return pltpu.CompilerParams(dimension_semantics=sem, vmem_limit_bytes=VMEM_LIMIT)


def _dot(a, b):
    return jnp.dot(a, b, preferred_element_type=F32)


def _dot_nt(a, b):
    return lax.dot_general(a, b, (((1,), (1,)), ((), ())), preferred_element_type=F32)


def _inproj_kernel(x_ref, g_ref, w_ref, waux_ref, gain_ref, flag_ref, o_ref, aux_ref, h_ref):
    j = pl.program_id(1)

    @pl.when(j == 0)
    def _():
        x = x_ref[...]
        ms = jnp.mean(x * x, axis=-1, keepdims=True)
        h = (x * lax.rsqrt(ms + RMS_EPS) * g_ref[...]).astype(BF16)
        h_ref[...] = h
        aux_ref[...] = _dot(h, waux_ref[...])

    acc = _dot(h_ref[...], w_ref[...])
    tn = acc.shape[1]
    for c in range(tn // HEAD_DIM):
        sl = slice(c * HEAD_DIM, (c + 1) * HEAD_DIM)
        y = acc[:, sl]
        ms = jnp.mean(y * y, axis=-1, keepdims=True)
        f = flag_ref[:, sl]
        inv = f * lax.rsqrt(ms + RMS_EPS) + (1.0 - f)
        o_ref[:, sl] = (y * inv * gain_ref[:, sl]).astype(o_ref.dtype)


def _inproj(x2, g, w_main, w_aux, gain, flag, tm, tn):
    m, d = x2.shape
    n = w_main.shape[1]
    return pl.pallas_call(
        _inproj_kernel,
        grid=(m // tm, n // tn),
        in_specs=[
            pl.BlockSpec((tm, d), lambda i, j: (i, 0)),
            pl.BlockSpec((1, d), lambda i, j: (0, 0)),
            pl.BlockSpec((d, tn), lambda i, j: (0, j)),
            pl.BlockSpec((d, LANES), lambda i, j: (0, 0)),
            pl.BlockSpec((1, tn), lambda i, j: (0, j)),
            pl.BlockSpec((1, tn), lambda i, j: (0, j)),
        ],
        out_specs=[
            pl.BlockSpec((tm, tn), lambda i, j: (i, j)),
            pl.BlockSpec((tm, LANES), lambda i, j: (i, 0)),
        ],
        out_shape=[
            jax.ShapeDtypeStruct((m, n), BF16),
            jax.ShapeDtypeStruct((m, LANES), F32),
        ],
        scratch_shapes=[pltpu.VMEM((tm, d), BF16)],
        compiler_params=_params("arbitrary", "arbitrary"),
        name="inproj",
    )(x2, g, w_main, w_aux, gain, flag)


def _gates_kernel(a_ref, b_ref, cum_ref, gate_ref, carry_ref):
    @pl.when(pl.program_id(1) == 0)
    def _():
        carry_ref[...] = jnp.zeros_like(carry_ref)

    a = a_ref[...]
    tc = a.shape[0]
    z = a + b_ref[...]
    lf = jnp.minimum(z, 0.0) - jnp.log(1.0 + jnp.exp(-jnp.abs(z)))
    rows = lax.broadcasted_iota(jnp.int32, (tc, tc), 0)
    cols = lax.broadcasted_iota(jnp.int32, (tc, tc), 1)
    tri = jnp.where(cols <= rows, 1.0, 0.0).astype(F32)
    cs = jnp.dot(tri, lf, preferred_element_type=F32, precision=lax.Precision.HIGHEST) + carry_ref[...]
    carry_ref[...] = cs[tc - 1:tc, :]
    cum_ref[...] = cs
    sg = 1.0 / (1.0 + jnp.exp(-a))
    for g in range(NSA_KV_HEADS):
        first = N_FOX_HEADS + g * NSA_GROUP * 3
        gate_ref[:, g * LANES:(g + 1) * LANES] = pltpu.roll(sg, shift=LANES - first, axis=1)


def _gates(aux, bias, tc):
    b, t, _ = aux.shape
    return pl.pallas_call(
        _gates_kernel,
        grid=(b, t // tc),
        in_specs=[
            pl.BlockSpec((None, tc, LANES), lambda bi, i: (bi, i, 0)),
            pl.BlockSpec((1, LANES), lambda bi, i: (0, 0)),
        ],
        out_specs=[
            pl.BlockSpec((None, tc, LANES), lambda bi, i: (bi, i, 0)),
            pl.BlockSpec((None, tc, NSA_KV_HEADS * LANES), lambda bi, i: (bi, i, 0)),
        ],
        out_shape=[
            jax.ShapeDtypeStruct((b, t, LANES), F32),
            jax.ShapeDtypeStruct((b, t, NSA_KV_HEADS * LANES), F32),
        ],
        scratch_shapes=[pltpu.VMEM((1, LANES), F32)],
        compiler_params=_params("arbitrary", "arbitrary"),
        name="gates",
    )(aux, bias)


def _compress_kernel(kc_ref, vc_ref, wk_ref, wv_ref, pk_ref, pv_ref, gk_ref, ko_ref, vo_ref):
    def comp(c_ref, w_ref, p_ref):
        a = c_ref[...]
        w = w_ref[...]
        half = w.shape[0] // 2
        wb = w.astype(BF16)
        y0 = _dot(a, wb[:half])
        y1 = _dot(a, wb[half:])
        y1 = pltpu.roll(y1, shift=y1.shape[0] - 1, axis=0)
        pw = jnp.dot(p_ref[...], w, preferred_element_type=F32, precision=lax.Precision.HIGHEST)
        return y0 + y1 + pw[0:1, :]

    k = comp(kc_ref, wk_ref, pk_ref)
    ms = jnp.mean(k * k, axis=-1, keepdims=True)
    ko_ref[...] = (k * lax.rsqrt(ms + RMS_EPS) * gk_ref[...]).astype(BF16)
    vo_ref[...] = comp(vc_ref, wv_ref, pv_ref).astype(BF16)


def _compress(kc_ch, vc_ch, wk, wv, pk, pv, gk):
    b, g, nch, cw = kc_ch.shape
    chunk_spec = pl.BlockSpec((None, None, nch, cw), lambda bi, gi: (bi, gi, 0, 0))
    w_spec = pl.BlockSpec(wk.shape, lambda bi, gi: (0, 0))
    p_spec = pl.BlockSpec(pk.shape, lambda bi, gi: (0, 0))
    o_spec = pl.BlockSpec((None, None, nch, HEAD_DIM), lambda bi, gi: (bi, gi, 0, 0))
    return pl.pallas_call(
        _compress_kernel,
        grid=(b, g),
        in_specs=[chunk_spec, chunk_spec, w_spec, w_spec, p_spec, p_spec,
                  pl.BlockSpec((1, HEAD_DIM), lambda bi, gi: (0, 0))],
        out_specs=[o_spec, o_spec],
        out_shape=[jax.ShapeDtypeStruct((b, g, nch, HEAD_DIM), BF16)] * 2,
        compiler_params=_params("arbitrary", "arbitrary"),
        name="compress",
    )(kc_ch, vc_ch, wk, wv, pk, pv, gk)


def _fox_kernel(q_ref, k_ref, v_ref, cq_ref, ck_ref, o_ref, *, tq):
    i = pl.program_id(2)
    q = q_ref[...]
    cq = cq_ref[...]
    rows = lax.broadcasted_iota(jnp.int32, (tq, tq), 0)
    cols = lax.broadcasted_iota(jnp.int32, (tq, tq), 1)

    def step(j, carry, diag):
        m, l, acc = carry
        k0 = pl.multiple_of(j * tq, tq)
        k = k_ref[pl.ds(k0, tq), :]
        v = v_ref[pl.ds(k0, tq), :]
        ck = ck_ref[:, pl.ds(k0, tq)]
        s = _dot_nt(q, k) + cq - ck
        if diag:
            s = jnp.where(cols <= rows, s, NEG_INF)
        m_new = jnp.maximum(m, jnp.max(s, axis=-1, keepdims=True))
        alpha = jnp.exp(m - m_new)
        p = jnp.exp(s - m_new)
        l = alpha * l + jnp.sum(p, axis=-1, keepdims=True)
        acc = alpha * acc + _dot(p.astype(BF16), v)
        return m_new, l, acc

    init = (jnp.full((tq, 1), NEG_INF, F32), jnp.zeros((tq, 1), F32), jnp.zeros((tq, HEAD_DIM), F32))
    carry = lax.fori_loop(0, i, lambda j, c: step(j, c, False), init)
    _, l, acc = step(i, carry, True)
    o_ref[...] = (acc / l).astype(o_ref.dtype)


def _fox(proj, cq, ck, tq):
    b, t, _ = proj.shape
    return pl.pallas_call(
        functools.partial(_fox_kernel, tq=tq),
        grid=(b, N_FOX_HEADS, t // tq),
        in_specs=[
            pl.BlockSpec((None, tq, HEAD_DIM), lambda bi, h, i: (bi, i, COL_QF + h)),
            pl.BlockSpec((None, t, HEAD_DIM), lambda bi, h, i: (bi, 0, COL_KF + h)),
            pl.BlockSpec((None, t, HEAD_DIM), lambda bi, h, i: (bi, 0, COL_VF + h)),
            pl.BlockSpec((None, None, tq, 1), lambda bi, h, i: (bi, h, i, 0)),
            pl.BlockSpec((None, None, 1, t), lambda bi, h, i: (bi, h, 0, 0)),
        ],
        out_specs=pl.BlockSpec((None, tq, HEAD_DIM), lambda bi, h, i: (bi, i, h)),
        out_shape=jax.ShapeDtypeStruct((b, t, FOX_W), BF16),
        compiler_params=_params("arbitrary", "arbitrary", "arbitrary"),
        name="fox",
    )(proj, proj, proj, cq, ck)


def _nsa_kernel(slopes_ref, q_ref, kc_ref, vc_ref, ks_ref, vs_ref, kw_ref, vw_ref, gate_ref, o_ref,
                m_ref, l_ref, acc_ref, *, tq, n_slc):
    g = pl.program_id(1)
    i = pl.program_id(2)
    q0 = i * tq
    ncp = kc_ref.shape[0]
    nsp = max(n_slc, LANES)
    R = NSA_GROUP
    slopes = [slopes_ref[g * R + r] for r in range(R)]
    qs = [q_ref[:, r * HEAD_DIM:(r + 1) * HEAD_DIM] for r in range(R)]

    kc = kc_ref[...]
    vc = vc_ref[...]
    n_i = lax.broadcasted_iota(jnp.int32, (tq, ncp), 1)
    t_i = q0 + lax.broadcasted_iota(jnp.int32, (tq, ncp), 0)
    dist_c = (t_i - (n_i * CMP_STRIDE + (CMP_BLOCK - 1))).astype(F32)
    mask_c = dist_c >= 0
    p_sum = jnp.zeros((tq, ncp), F32)
    o_cmp = []
    for r in range(R):
        s = _dot_nt(qs[r], kc) - slopes[r] * dist_c
        s = jnp.where(mask_c, s, NEG_INF)
        m = jnp.max(s, axis=-1, keepdims=True)
        p = jnp.where(mask_c, jnp.exp(s - m), 0.0)
        p = p / jnp.maximum(jnp.sum(p, axis=-1, keepdims=True), 1e-30)
        o_cmp.append(_dot(p.astype(BF16), vc))
        p_sum = p_sum + p

    cn = lax.broadcasted_iota(jnp.int32, (ncp, nsp), 0)
    sj = lax.broadcasted_iota(jnp.int32, (ncp, nsp), 1)
    ov = (cn * CMP_STRIDE < (sj + 1) * SLC_BLOCK) & (cn * CMP_STRIDE + CMP_BLOCK > sj * SLC_BLOCK)
    ov = jnp.where(ov & (cn < ncp - 1) & (sj < n_slc), 1.0, 0.0).astype(F32)
    imp = jnp.dot(p_sum, ov, preferred_element_type=F32, precision=lax.Precision.HIGHEST)
    blk = lax.broadcasted_iota(jnp.int32, (tq, nsp), 1)
    tpos = q0 + lax.broadcasted_iota(jnp.int32, (tq, nsp), 0)
    cur = tpos // SLC_BLOCK
    forced = (blk == 0) | (blk == cur) | (blk == cur - 1)
    valid = blk * SLC_BLOCK <= tpos
    work = jnp.where(forced, FORCE_SCORE, jnp.where(valid, imp, -FORCE_SCORE))
    sel = jnp.zeros((tq, nsp), F32)
    for _ in range(min(SLC_TOPK, n_slc)):
        mx = jnp.max(work, axis=-1, keepdims=True)
        first = jnp.min(jnp.where(work == mx, blk, nsp), axis=-1, keepdims=True)
        pick = blk == first
        sel = jnp.where(pick, 1.0, sel)
        work = jnp.where(pick, -jnp.inf, work)
    sel_b = sel.astype(BF16)

    rows = lax.broadcasted_iota(jnp.int32, (tq, tq), 0)
    cols = lax.broadcasted_iota(jnp.int32, (tq, tq), 1)
    e_blk = lax.broadcasted_iota(jnp.int32, (nsp, tq), 0)
    e_key = lax.broadcasted_iota(jnp.int32, (nsp, tq), 1)

    def reset():
        m_ref[...] = jnp.full(m_ref.shape, NEG_INF, F32)
        l_ref[...] = jnp.zeros(l_ref.shape, F32)
        acc_ref[...] = jnp.zeros(acc_ref.shape, F32)

    def update(k, v, mask, dist):
        for r in range(R):
            s = _dot_nt(qs[r], k) - slopes[r] * dist
            s = jnp.where(mask, s, NEG_INF)
            m_old = m_ref[r]
            m_new = jnp.maximum(m_old, jnp.max(s, axis=-1, keepdims=True))
            alpha = jnp.exp(m_old - m_new)
            p = jnp.where(mask, jnp.exp(s - m_new), 0.0)
            l_ref[r] = alpha * l_ref[r] + jnp.sum(p, axis=-1, keepdims=True)
            acc_ref[r] = alpha * acc_ref[r] + _dot(p.astype(BF16), v)
            m_ref[r] = m_new

    def result():
        return [acc_ref[r] / jnp.maximum(l_ref[r], 1e-30) for r in range(R)]

    reset()

    def slc_step(j, _):
        k0 = pl.multiple_of(j * tq, tq)
        expand = jnp.where(e_blk == (k0 + e_key) // SLC_BLOCK, 1.0, 0.0).astype(BF16)
        hit = _dot(sel_b, expand) > 0.5
        dist = q0 - k0 + rows - cols
        mask = hit & (dist >= 0)
        update(ks_ref[pl.ds(k0, tq), :], vs_ref[pl.ds(k0, tq), :], mask, dist.astype(F32))
        return 0

    lax.fori_loop(0, i + 1, slc_step, 0)
    o_slc = result()

    reset()

    def win_step(j, _):
        k0 = pl.multiple_of(j * tq, tq)
        dist = q0 - k0 + rows - cols
        mask = (dist >= 0) & (dist < WINDOW)
        update(kw_ref[pl.ds(k0, tq), :], vw_ref[pl.ds(k0, tq), :], mask, dist.astype(F32))
        return 0

    lax.fori_loop(jnp.maximum(i - WINDOW // tq, 0), i + 1, win_step, 0)
    o_win = result()

    gt = gate_ref[...]
    for r in range(R):
        o = (gt[:, 3 * r:3 * r + 1] * o_cmp[r] + gt[:, 3 * r + 1:3 * r + 2] * o_slc[r]
             + gt[:, 3 * r + 2:3 * r + 3] * o_win[r])
        o_ref[:, r * HEAD_DIM:(r + 1) * HEAD_DIM] = o.astype(o_ref.dtype)


def _nsa(slopes, proj, k_cmp, v_cmp, gates, tq):
    b, t, _ = proj.shape
    ncp = k_cmp.shape[2]
    gw = NSA_GROUP * HEAD_DIM
    kv = lambda col: pl.BlockSpec((None, t, HEAD_DIM), lambda bi, g, i, s: (bi, 0, col + g))
    cmp_spec = pl.BlockSpec((None, None, ncp, HEAD_DIM), lambda bi, g, i, s: (bi, g, 0, 0))
    grid_spec = pltpu.PrefetchScalarGridSpec(
        num_scalar_prefetch=1,
        grid=(b, NSA_KV_HEADS, t // tq),
        in_specs=[
            pl.BlockSpec((None, tq, gw), lambda bi, g, i, s: (bi, i, COL_QN // NSA_GROUP + g)),
            cmp_spec, cmp_spec,
            kv(COL_KS), kv(COL_VS), kv(COL_KW), kv(COL_VW),
            pl.BlockSpec((None, tq, LANES), lambda bi, g, i, s: (bi, i, g)),
        ],
        out_specs=pl.BlockSpec((None, tq, gw), lambda bi, g, i, s: (bi, i, g)),
        scratch_shapes=[
            pltpu.VMEM((NSA_GROUP, tq, 1), F32),
            pltpu.VMEM((NSA_GROUP, tq, 1), F32),
            pltpu.VMEM((NSA_GROUP, tq, HEAD_DIM), F32),
        ],
    )
    return pl.pallas_call(
        functools.partial(_nsa_kernel, tq=tq, n_slc=t // SLC_BLOCK),
        grid_spec=grid_spec,
        out_shape=jax.ShapeDtypeStruct((b, t, NSA_W), BF16),
        compiler_params=_params("arbitrary", "arbitrary", "arbitrary"),
        name="nsa",
    )(slopes, proj, k_cmp, v_cmp, proj, proj, proj, proj, gates)


def _outproj_kernel(x_ref, of_ref, on_ref, wf_ref, wn_ref, o_ref):
    o_ref[...] = x_ref[...] + _dot(of_ref[...], wf_ref[...]) + _dot(on_ref[...], wn_ref[...])


def _outproj(x2, o_fox, o_nsa, w_fox, w_nsa, tm, tn):
    m, d = x2.shape
    return pl.pallas_call(
        _outproj_kernel,
        grid=(m // tm, d // tn),
        in_specs=[
            pl.BlockSpec((tm, tn), lambda i, j: (i, j)),
            pl.BlockSpec((tm, FOX_W), lambda i, j: (i, 0)),
            pl.BlockSpec((tm, NSA_W), lambda i, j: (i, 0)),
            pl.BlockSpec((FOX_W, tn), lambda i, j: (0, j)),
            pl.BlockSpec((NSA_W, tn), lambda i, j: (0, j)),
        ],
        out_specs=pl.BlockSpec((tm, tn), lambda i, j: (i, j)),
        out_shape=jax.ShapeDtypeStruct((m, d), F32),
        compiler_params=_params("arbitrary", "arbitrary"),
        name="outproj",
    )(x2, o_fox, o_nsa, w_fox, w_nsa)


def _ffn_up_kernel(x_ref, g_ref, wg_ref, wu_ref, o_ref, h_ref):
    @pl.when(pl.program_id(1) == 0)
    def _():
        x = x_ref[...]
        ms = jnp.mean(x * x, axis=-1, keepdims=True)
        h_ref[...] = (x * lax.rsqrt(ms + RMS_EPS) * g_ref[...]).astype(BF16)

    h = h_ref[...]
    a = _dot(h, wg_ref[...])
    u = _dot(h, wu_ref[...])
    o_ref[...] = (a / (1.0 + jnp.exp(-a)) * u).astype(o_ref.dtype)


def _ffn_up(x2, g, w_gate, w_up, tm, tn):
    m, d = x2.shape
    n = w_gate.shape[1]
    return pl.pallas_call(
        _ffn_up_kernel,
        grid=(m // tm, n // tn),
        in_specs=[
            pl.BlockSpec((tm, d), lambda i, j: (i, 0)),
            pl.BlockSpec((1, d), lambda i, j: (0, 0)),
            pl.BlockSpec((d, tn), lambda i, j: (0, j)),
            pl.BlockSpec((d, tn), lambda i, j: (0, j)),
        ],
        out_specs=pl.BlockSpec((tm, tn), lambda i, j: (i, j)),
        out_shape=jax.ShapeDtypeStruct((m, n), BF16),
        scratch_shapes=[pltpu.VMEM((tm, d), BF16)],
        compiler_params=_params("arbitrary", "arbitrary"),
        name="ffn_up",
    )(x2, g, w_gate, w_up)


def _ffn_down_kernel(x_ref, a_ref, w_ref, o_ref):
    o_ref[...] = x_ref[...] + _dot(a_ref[...], w_ref[...])


def _ffn_down(x2, act, w_down, tm, tn):
    m, d = x2.shape
    f = act.shape[1]
    return pl.pallas_call(
        _ffn_down_kernel,
        grid=(m // tm, d // tn),
        in_specs=[
            pl.BlockSpec((tm, tn), lambda i, j: (i, j)),
            pl.BlockSpec((tm, f), lambda i, j: (i, 0)),
            pl.BlockSpec((f, tn), lambda i, j: (0, j)),
        ],
        out_specs=pl.BlockSpec((tm, tn), lambda i, j: (i, j)),
        out_shape=jax.ShapeDtypeStruct((m, d), F32),
        compiler_params=_params("arbitrary", "arbitrary"),
        name="ffn_down",
    )(x2, act, w_down)


def _layer_weights(w_in, fox_forget_bias, fox_q_norm, fox_k_norm, nsa_q_norm, slc_k_norm, win_k_norm):
    splits = (FOX_W, FOX_W, FOX_W, N_FOX_HEADS, NSA_W, KV_W, KV_W, KV_W, KV_W, KV_W, KV_W, 3 * N_NSA_HEADS)
    cuts = [0]
    for s in splits:
        cuts.append(cuts[-1] + s)
    part = lambda k: w_in[:, cuts[k]:cuts[k + 1]]
    w_main = jnp.concatenate([part(k) for k in (0, 1, 2, 4, 5, 6, 7, 8, 9, 10)], axis=1).astype(BF16)
    d = w_in.shape[0]
    w_aux = jnp.concatenate(
        [part(3), part(11), jnp.zeros((d, LANES - N_FOX_HEADS - 3 * N_NSA_HEADS), w_in.dtype)], axis=1).astype(BF16)

    scale = HEAD_DIM ** -0.5
    ones = lambda n: jnp.ones((n * HEAD_DIM,), F32)
    rep = lambda gvec, n, s=1.0: jnp.tile(gvec.astype(F32) * s, n)
    gain = jnp.concatenate([
        rep(fox_q_norm, N_FOX_HEADS, scale), rep(fox_k_norm, N_FOX_HEADS), ones(N_FOX_HEADS),
        rep(nsa_q_norm, N_NSA_HEADS, scale), ones(2 * NSA_KV_HEADS),
        rep(slc_k_norm, NSA_KV_HEADS), ones(NSA_KV_HEADS),
        rep(win_k_norm, NSA_KV_HEADS), ones(NSA_KV_HEADS)])[None, :]
    z = lambda n: jnp.zeros((n * HEAD_DIM,), F32)
    flag = jnp.concatenate([
        ones(2 * N_FOX_HEADS), z(N_FOX_HEADS), ones(N_NSA_HEADS), z(2 * NSA_KV_HEADS),
        ones(NSA_KV_HEADS), z(NSA_KV_HEADS), ones(NSA_KV_HEADS), z(NSA_KV_HEADS)])[None, :]
    bias = jnp.concatenate([fox_forget_bias.astype(F32), jnp.zeros((LANES - N_FOX_HEADS,), F32)])[None, :]
    return w_main, w_aux, gain, flag, bias


def _pick(n, prefs):
    for p in prefs:
        if n % p == 0:
            return p
    return n


def kernel(x, attn_norm, w_in, fox_forget_bias, fox_q_norm, fox_k_norm, nsa_q_norm, cmp_k_norm, slc_k_norm,
           win_k_norm, cmp_pos_k, cmp_pos_v, cmp_w_k, cmp_w_v, w_out, ffn_norm, w_gate, w_up, w_down):
    b, t, d = x.shape
    depth = w_in.shape[0]
    m = b * t
    tm = _pick(m, (1024, 512, 256, 128))
    nch = t // CMP_STRIDE
    slopes = jnp.asarray([2.0 ** (-8.0 * (i + 1) / N_NSA_HEADS) for i in range(N_NSA_HEADS)], F32)

    x2 = x.reshape(m, d)
    for l in range(depth):
        w_main, w_aux, gain, flag, bias = _layer_weights(
            w_in[l], fox_forget_bias[l], fox_q_norm[l], fox_k_norm[l], nsa_q_norm[l], slc_k_norm[l], win_k_norm[l])
        proj, aux = _inproj(x2, attn_norm[l][None, :], w_main, w_aux, gain, flag, tm, 512)
        proj = proj.reshape(b, t, MAIN_W)
        cum, gates = _gates(aux.reshape(b, t, LANES), bias, _pick(t, (512, 256, 128)))
        cum_h = jnp.transpose(cum[:, :, :N_FOX_HEADS], (0, 2, 1))

        def chunks(col):
            c = proj[:, :, col * HEAD_DIM:(col + NSA_KV_HEADS) * HEAD_DIM]
            c = c.reshape(b, nch, CMP_STRIDE, NSA_KV_HEADS, HEAD_DIM)
            return jnp.transpose(c, (0, 3, 1, 2, 4)).reshape(b, NSA_KV_HEADS, nch, CMP_STRIDE * HEAD_DIM)

        pos = lambda p: jnp.broadcast_to(p.reshape(1, CMP_BLOCK * HEAD_DIM), (8, CMP_BLOCK * HEAD_DIM))
        k_cmp, v_cmp = _compress(
            chunks(COL_KC), chunks(COL_VC),
            cmp_w_k[l].reshape(CMP_BLOCK * HEAD_DIM, HEAD_DIM), cmp_w_v[l].reshape(CMP_BLOCK * HEAD_DIM, HEAD_DIM),
            pos(cmp_pos_k[l]), pos(cmp_pos_v[l]), cmp_k_norm[l][None, :].astype(F32))

        o_fox = _fox(proj, cum_h[:, :, :, None], cum_h[:, :, None, :], _pick(t, (512, 256, 128)))
        o_nsa = _nsa(slopes, proj, k_cmp, v_cmp, gates, 128)

        wo = w_out[l].astype(BF16)
        x2 = _outproj(x2, o_fox.reshape(m, FOX_W), o_nsa.reshape(m, NSA_W), wo[:FOX_W], wo[FOX_W:], tm, 1024)

        act = _ffn_up(x2, ffn_norm[l][None, :], w_gate[l].astype(BF16), w_up[l].astype(BF16), tm, 512)
        x2 = _ffn_down(x2, act, w_down[l].astype(BF16), _pick(m, (512, 256, 128)), 1024)
    return x2.reshape(b, t, d)
```

```python
import functools

import jax
import jax.numpy as jnp
from jax import lax
from jax.experimental import pallas as pl
from jax.experimental.pallas import tpu as pltpu

HEAD_DIM = 128
N_FOX_HEADS = 8
N_NSA_HEADS = 8
NSA_GROUP = 4
NSA_KV_HEADS = N_NSA_HEADS // NSA_GROUP
CMP_BLOCK = 32
CMP_STRIDE = 16
SLC_BLOCK = 64
SLC_TOPK = 16
WINDOW = 512
RMS_EPS = 1e-6
NEG_INF = -1e30
FORCE_SCORE = 1e9

FOX_W = N_FOX_HEADS * HEAD_DIM
NSA_W = N_NSA_HEADS * HEAD_DIM
KV_W = NSA_KV_HEADS * HEAD_DIM
LANES = 128
VMEM_LIMIT = 56 * 1024 * 1024
CUM_PIECES = 3

F32 = jnp.float32
BF16 = jnp.bfloat16

COL_QF = 0
COL_KF = COL_QF + N_FOX_HEADS
COL_VF = COL_KF + N_FOX_HEADS
COL_QN = COL_VF + N_FOX_HEADS
COL_KC = COL_QN + N_NSA_HEADS
COL_VC = COL_KC + NSA_KV_HEADS
COL_KS = COL_VC + NSA_KV_HEADS
COL_VS = COL_KS + NSA_KV_HEADS
COL_KW = COL_VS + NSA_KV_HEADS
COL_VW = COL_KW + NSA_KV_HEADS
MAIN_W = (COL_VW + NSA_KV_HEADS) * HEAD_DIM


def _params(*sem):
    return pltpu.CompilerParams(dimension_semantics=sem, vmem_limit_bytes=VMEM_LIMIT)


def _dot(a, b):
    return jnp.dot(a, b, preferred_element_type=F32)


def _inproj_kernel(x_ref, g_ref, w_ref, waux_ref, gain_ref, flag_ref, o_ref, aux_ref, h_ref):
    j = pl.program_id(1)

    @pl.when(j == 0)
    def _():
        x = x_ref[...]
        ms = jnp.mean(x * x, axis=-1, keepdims=True)
        h = (x * lax.rsqrt(ms + RMS_EPS) * g_ref[...]).astype(BF16)
        h_ref[...] = h
        aux_ref[...] = _dot(h, waux_ref[...])

    acc = _dot(h_ref[...], w_ref[...])
    tn = acc.shape[1]
    for c in range(tn // HEAD_DIM):
        sl = slice(c * HEAD_DIM, (c + 1) * HEAD_DIM)
        y = acc[:, sl]
        ms = jnp.mean(y * y, axis=-1, keepdims=True)
        f = flag_ref[:, sl]
        inv = f * lax.rsqrt(ms + RMS_EPS) + (1.0 - f)
        o_ref[:, sl] = (y * inv * gain_ref[:, sl]).astype(o_ref.dtype)


def _inproj(x2, g, w_main, w_aux, gain, flag, tm, tn):
    m, d = x2.shape
    n = w_main.shape[1]
    return pl.pallas_call(
        _inproj_kernel,
        grid=(m // tm, n // tn),
        in_specs=[
            pl.BlockSpec((tm, d), lambda i, j: (i, 0)),
            pl.BlockSpec((1, d), lambda i, j: (0, 0)),
            pl.BlockSpec((d, tn), lambda i, j: (0, j)),
            pl.BlockSpec((d, LANES), lambda i, j: (0, 0)),
            pl.BlockSpec((1, tn), lambda i, j: (0, j)),
            pl.BlockSpec((1, tn), lambda i, j: (0, j)),
        ],
        out_specs=[
            pl.BlockSpec((tm, tn), lambda i, j: (i, j)),
            pl.BlockSpec((tm, LANES), lambda i, j: (i, 0)),
        ],
        out_shape=[
            jax.ShapeDtypeStruct((m, n), BF16),
            jax.ShapeDtypeStruct((m, LANES), F32),
        ],
        scratch_shapes=[pltpu.VMEM((tm, d), BF16)],
        compiler_params=_params("arbitrary", "arbitrary"),
        name="inproj",
    )(x2, g, w_main, w_aux, gain, flag)


def _gates_kernel(a_ref, b_ref, cum_ref, gate_ref, carry_ref):
    @pl.when(pl.program_id(1) == 0)
    def _():
        carry_ref[...] = jnp.zeros_like(carry_ref)

    a = a_ref[...]
    tc = a.shape[0]
    z = a + b_ref[...]
    lf = jnp.minimum(z, 0.0) - jnp.log(1.0 + jnp.exp(-jnp.abs(z)))
    rows = lax.broadcasted_iota(jnp.int32, (tc, tc), 0)
    cols = lax.broadcasted_iota(jnp.int32, (tc, tc), 1)
    tri = jnp.where(cols <= rows, 1.0, 0.0).astype(F32)
    cs = jnp.dot(tri, lf, preferred_element_type=F32, precision=lax.Precision.HIGHEST) + carry_ref[...]
    carry_ref[...] = cs[tc - 1:tc, :]
    rest = -cs
    for k in range(CUM_PIECES):
        piece = rest.astype(BF16)
        cum_ref[:, k * LANES:(k + 1) * LANES] = piece
        rest = rest - piece.astype(F32)
    sg = 1.0 / (1.0 + jnp.exp(-a))
    for g in range(NSA_KV_HEADS):
        first = N_FOX_HEADS + g * NSA_GROUP * 3
        gate_ref[:, g * LANES:(g + 1) * LANES] = pltpu.roll(sg, shift=LANES - first, axis=1)


def _gates(aux, bias, tc):
    b, t, _ = aux.shape
    return pl.pallas_call(
        _gates_kernel,
        grid=(b, t // tc),
        in_specs=[
            pl.BlockSpec((None, tc, LANES), lambda bi, i: (bi, i, 0)),
            pl.BlockSpec((1, LANES), lambda bi, i: (0, 0)),
        ],
        out_specs=[
            pl.BlockSpec((None, tc, CUM_PIECES * LANES), lambda bi, i: (bi, i, 0)),
            pl.BlockSpec((None, tc, NSA_KV_HEADS * LANES), lambda bi, i: (bi, i, 0)),
        ],
        out_shape=[
            jax.ShapeDtypeStruct((b, t, CUM_PIECES * LANES), BF16),
            jax.ShapeDtypeStruct((b, t, NSA_KV_HEADS * LANES), F32),
        ],
        scratch_shapes=[pltpu.VMEM((1, LANES), F32)],
        compiler_params=_params("arbitrary", "arbitrary"),
        name="gates",
    )(aux, bias)


def _compress_kernel(kc_ref, vc_ref, wk_ref, wv_ref, pk_ref, pv_ref, gk_ref, ko_ref, vo_ref):
    def comp(c_ref, w_ref, p_ref):
        a = c_ref[...]
        w = w_ref[...]
        half = w.shape[0] // 2
        wb = w.astype(BF16)
        y0 = _dot(a, wb[:half])
        y1 = _dot(a, wb[half:])
        y1 = pltpu.roll(y1, shift=y1.shape[0] - 1, axis=0)
        pw = jnp.dot(p_ref[...], w, preferred_element_type=F32, precision=lax.Precision.HIGHEST)
        return y0 + y1 + pw[0:1, :]

    k = comp(kc_ref, wk_ref, pk_ref)
    ms = jnp.mean(k * k, axis=-1, keepdims=True)
    ko_ref[...] = (k * lax.rsqrt(ms + RMS_EPS) * gk_ref[...]).astype(BF16)
    vo_ref[...] = comp(vc_ref, wv_ref, pv_ref).T.astype(BF16)


def _compress(kc_ch, vc_ch, wk, wv, pk, pv, gk):
    b, g, nch, cw = kc_ch.shape
    chunk_spec = pl.BlockSpec((None, None, nch, cw), lambda bi, gi: (bi, gi, 0, 0))
    w_spec = pl.BlockSpec(wk.shape, lambda bi, gi: (0, 0))
    p_spec = pl.BlockSpec(pk.shape, lambda bi, gi: (0, 0))
    o_spec = pl.BlockSpec((None, None, nch, HEAD_DIM), lambda bi, gi: (bi, gi, 0, 0))
    return pl.pallas_call(
        _compress_kernel,
        grid=(b, g),
        in_specs=[chunk_spec, chunk_spec, w_spec, w_spec, p_spec, p_spec,
                  pl.BlockSpec((1, HEAD_DIM), lambda bi, gi: (0, 0))],
        out_specs=[o_spec, pl.BlockSpec((None, None, HEAD_DIM, nch), lambda bi, gi: (bi, gi, 0, 0))],
        out_shape=[jax.ShapeDtypeStruct((b, g, nch, HEAD_DIM), BF16), jax.ShapeDtypeStruct((b, g, HEAD_DIM, nch), BF16)],
        compiler_params=_params("arbitrary", "arbitrary"),
        name="compress",
    )(kc_ch, vc_ch, wk, wv, pk, pv, gk)


def _transpose_into(dst_ref, rows, col0, x):
    for c in range(x.shape[0] // LANES):
        blk = x[c * LANES:(c + 1) * LANES, :].astype(F32).T
        dst_ref[rows, col0 + c * LANES:col0 + (c + 1) * LANES] = blk.astype(dst_ref.dtype)


def _online_update(s, v_t, m_ref, l_ref, acc_ref):
    m_old = m_ref[...]
    m_new = jnp.maximum(m_old, jnp.max(s, axis=0, keepdims=True))
    alpha = jnp.exp(m_old - m_new)
    p = jnp.exp(s - m_new)
    l_ref[...] = alpha * l_ref[...] + jnp.sum(p, axis=0, keepdims=True)
    acc_ref[...] = alpha * acc_ref[...] + _dot(v_t, p.astype(BF16))
    m_ref[...] = m_new


def _reset(m_ref, l_ref, acc_ref):
    m_ref[...] = jnp.full(m_ref.shape, NEG_INF, F32)
    l_ref[...] = jnp.zeros(l_ref.shape, F32)
    acc_ref[...] = jnp.zeros(acc_ref.shape, F32)


FOX_TQ = 512
FOX_TK = 256


def _fox_kernel(q_ref, kx_ref, vt_ref, o_ref, qt_ref, s_ref, m_ref, l_ref, acc_ref, *, tq, tk):
    i = pl.program_id(2)
    _transpose_into(qt_ref, slice(0, HEAD_DIM), 0, q_ref[...])
    row = lax.broadcasted_iota(jnp.int32, (HEAD_DIM, tq), 0)
    qt_ref[HEAD_DIM:, :] = jnp.where(row < CUM_PIECES, 1.0, 0.0).astype(BF16)
    _reset(m_ref, l_ref, acc_ref)

    def scores(j, slot):
        k0 = pl.multiple_of(j * tk, tk)
        s_ref[slot] = _dot(kx_ref[pl.ds(k0, tk), :], qt_ref[...])

    def consume(j, slot, diag):
        k0 = pl.multiple_of(j * tk, tk)
        s = s_ref[slot]
        if diag:
            key = k0 + lax.broadcasted_iota(jnp.int32, (tk, tq), 0)
            t = i * tq + lax.broadcasted_iota(jnp.int32, (tk, tq), 1)
            s = jnp.where(key <= t, s, NEG_INF)
        _online_update(s, vt_ref[:, pl.ds(k0, tk)], m_ref, l_ref, acc_ref)

    n = tq // tk
    scores(0, 0)

    def pair(jj, c):
        scores(2 * jj + 1, 1)
        consume(2 * jj, 0, False)
        scores(2 * jj + 2, 0)
        consume(2 * jj + 1, 1, False)
        return c

    lax.fori_loop(0, i * (n // 2), pair, 0)
    for d in range(n):
        if d + 1 < n:
            scores(i * n + d + 1, (d + 1) % 2)
        consume(i * n + d, d % 2, True)
    o = acc_ref[...] * (1.0 / l_ref[...])
    for c in range(tq // LANES):
        o_ref[c * LANES:(c + 1) * LANES, :] = o[:, c * LANES:(c + 1) * LANES].T.astype(o_ref.dtype)


def _fox(proj, kx, vt, tq, tk):
    b, t, _ = proj.shape
    assert tq % (2 * tk) == 0
    return pl.pallas_call(
        functools.partial(_fox_kernel, tq=tq, tk=tk),
        grid=(b, N_FOX_HEADS, t // tq),
        in_specs=[
            pl.BlockSpec((None, tq, HEAD_DIM), lambda bi, h, i: (bi, i, COL_QF + h)),
            pl.BlockSpec((None, None, t, 2 * HEAD_DIM), lambda bi, h, i: (bi, h, 0, 0)),
            pl.BlockSpec((None, None, HEAD_DIM, t), lambda bi, h, i: (bi, h, 0, 0)),
        ],
        out_specs=pl.BlockSpec((None, tq, HEAD_DIM), lambda bi, h, i: (bi, i, h)),
        out_shape=jax.ShapeDtypeStruct((b, t, FOX_W), BF16),
        scratch_shapes=[
            pltpu.VMEM((2 * HEAD_DIM, tq), BF16),
            pltpu.VMEM((2, tk, tq), F32),
            pltpu.VMEM((1, tq), F32),
            pltpu.VMEM((1, tq), F32),
            pltpu.VMEM((HEAD_DIM, tq), F32),
        ],
        compiler_params=_params("arbitrary", "arbitrary", "arbitrary"),
        name="fox",
    )(proj, kx, vt)


NSA_TQ = 128
SLC_TK = 256
MASK_BIAS = -1e9


def _nsa_kernel(slopes_ref, q_ref, kc_ref, vct_ref, ksx_ref, vst_ref, kw_ref, vwt_ref, gate_ref, o_ref,
                qt_ref, s_ref, m_ref, l_ref, acc_ref, tile_ref, *, tq, n_slc):
    g = pl.program_id(1)
    i = pl.program_id(2)
    q0 = i * tq
    R = NSA_GROUP
    rt = R * tq
    ncp = kc_ref.shape[0]
    nsp = LANES
    slopes = [slopes_ref[g * R + r] for r in range(R)]

    for r in range(R):
        _transpose_into(qt_ref, slice(0, HEAD_DIM), r * tq, q_ref[:, r * HEAD_DIM:(r + 1) * HEAD_DIM])
    qt = qt_ref[:HEAD_DIM, :]

    col = lax.broadcasted_iota(jnp.int32, (1, rt), 1)
    rid = col // tq
    slope_row = jnp.where(rid == 0, slopes[0], jnp.where(rid == 1, slopes[1], jnp.where(rid == 2, slopes[2], slopes[3])))
    t_row = q0 + col - rid * tq

    cmp_end = lax.broadcasted_iota(jnp.int32, (ncp, rt), 0) * CMP_STRIDE + (CMP_BLOCK - 1)
    dist_c = (t_row - cmp_end).astype(F32)
    mask_c = dist_c >= 0
    s = jnp.where(mask_c, _dot(kc_ref[...], qt) - slope_row * dist_c, NEG_INF)
    mx = jnp.max(s, axis=0, keepdims=True)
    p = jnp.where(mask_c, jnp.exp(s - mx), 0.0)
    p = p * (1.0 / jnp.maximum(jnp.sum(p, axis=0, keepdims=True), 1e-30))
    o_cmp = _dot(vct_ref[...], p.astype(BF16))
    p_sum = p[:, 0:tq]
    for r in range(1, R):
        p_sum = p_sum + p[:, r * tq:(r + 1) * tq]

    sj = lax.broadcasted_iota(jnp.int32, (nsp, ncp), 0)
    cn = lax.broadcasted_iota(jnp.int32, (nsp, ncp), 1)
    ov = (cn * CMP_STRIDE < (sj + 1) * SLC_BLOCK) & (cn * CMP_STRIDE + CMP_BLOCK > sj * SLC_BLOCK)
    ov = jnp.where(ov & (cn < ncp - 1) & (sj < n_slc), 1.0, 0.0).astype(F32)
    imp = jnp.dot(ov, p_sum, preferred_element_type=F32, precision=lax.Precision.HIGHEST)
    blk = lax.broadcasted_iota(jnp.int32, (nsp, tq), 0)
    tpos = q0 + lax.broadcasted_iota(jnp.int32, (nsp, tq), 1)
    cur = tpos // SLC_BLOCK
    forced = (blk == 0) | (blk == cur) | (blk == cur - 1)
    valid = blk * SLC_BLOCK <= tpos
    work = jnp.where(forced, FORCE_SCORE, jnp.where(valid, imp, -FORCE_SCORE))
    sel = jnp.zeros((nsp, tq), F32)
    for _ in range(min(SLC_TOPK, n_slc)):
        mx = jnp.max(work, axis=0, keepdims=True)
        first = jnp.min(jnp.where(work == mx, blk, nsp), axis=0, keepdims=True)
        pick = blk == first
        sel = jnp.where(pick, 1.0, sel)
        work = jnp.where(pick, -jnp.inf, work)

    blk_f = blk.astype(F32)
    for r in range(R):
        aug = jnp.where(blk == 0, slopes[r],
                        jnp.where(sel > 0.5, slopes[r] * SLC_BLOCK * blk_f, MASK_BIAS))
        qt_ref[HEAD_DIM:, r * tq:(r + 1) * tq] = aug.astype(BF16)

    bpt = SLC_TK // SLC_BLOCK
    jd = q0 // SLC_TK
    hit = jnp.max(sel.T, axis=0, keepdims=True)
    shift = 1
    while shift < bpt:
        hit = jnp.maximum(hit, pltpu.roll(hit, shift=nsp - shift, axis=1))
        shift *= 2
    cnt = jnp.int32(0)
    for j in range(n_slc // bpt):
        tile_ref[cnt] = j
        cnt = cnt + ((hit[0, j * bpt] > 0.5) & (j < jd)).astype(jnp.int32)
    tile_ref[cnt] = jd

    _reset(m_ref, l_ref, acc_ref)

    def scores(idx, slot):
        k0 = pl.multiple_of(tile_ref[idx] * SLC_TK, SLC_TK)
        s_ref[slot] = _dot(ksx_ref[pl.ds(k0, SLC_TK), :], qt_ref[...])

    def consume(idx, slot, diag):
        k0 = pl.multiple_of(tile_ref[idx] * SLC_TK, SLC_TK)
        s = s_ref[slot]
        if diag:
            key = k0 + lax.broadcasted_iota(jnp.int32, (SLC_TK, rt), 0)
            s = jnp.where(key <= t_row, s, NEG_INF)
        _online_update(s, vst_ref[:, pl.ds(k0, SLC_TK)], m_ref, l_ref, acc_ref)

    scores(0, 0)

    def pair(jj, c):
        scores(2 * jj + 1, 1)
        consume(2 * jj, 0, False)
        scores(2 * jj + 2, 0)
        consume(2 * jj + 1, 1, False)
        return c

    lax.fori_loop(0, cnt // 2, pair, 0)

    @pl.when(cnt % 2 == 1)
    def _():
        scores(cnt, 1)
        consume(cnt - 1, 0, False)
        consume(cnt, 1, True)

    @pl.when(cnt % 2 == 0)
    def _():
        consume(cnt, 0, True)

    o_slc = acc_ref[...]
    inv_l_slc = 1.0 / l_ref[...]

    span = WINDOW + tq
    w0 = pl.multiple_of(jnp.maximum(q0 - WINDOW, 0), tq)
    dist_w = t_row - (w0 + lax.broadcasted_iota(jnp.int32, (span, rt), 0))
    mask_w = (dist_w >= 0) & (dist_w < WINDOW)
    s = jnp.where(mask_w, _dot(kw_ref[pl.ds(w0, span), :], qt) - slope_row * dist_w.astype(F32), NEG_INF)
    mx = jnp.max(s, axis=0, keepdims=True)
    p = jnp.exp(s - mx)
    o_win = _dot(vwt_ref[:, pl.ds(w0, span)], p.astype(BF16))
    inv_l_win = 1.0 / jnp.sum(p, axis=0, keepdims=True)

    gt = gate_ref[...].T
    for r in range(R):
        cols = slice(r * tq, (r + 1) * tq)
        o = (gt[3 * r:3 * r + 1] * o_cmp[:, cols]
             + (gt[3 * r + 1:3 * r + 2] * inv_l_slc[:, cols]) * o_slc[:, cols]
             + (gt[3 * r + 2:3 * r + 3] * inv_l_win[:, cols]) * o_win[:, cols])
        o_ref[:, r * HEAD_DIM:(r + 1) * HEAD_DIM] = o.T.astype(o_ref.dtype)


def _nsa(slopes, proj, k_cmp, v_cmp_t, ksx, vs_t, vw_t, gates, tq):
    b, t, _ = proj.shape
    n_slc = t // SLC_BLOCK
    assert n_slc <= LANES and t % SLC_TK == 0 and t >= WINDOW + tq and SLC_TK % tq == 0 and tq == LANES
    ncp = k_cmp.shape[2]
    gw = NSA_GROUP * HEAD_DIM
    rt = NSA_GROUP * tq
    per_group = lambda shape: pl.BlockSpec((None, None) + shape, lambda bi, g, i, s: (bi, g, 0, 0))
    grid_spec = pltpu.PrefetchScalarGridSpec(
        num_scalar_prefetch=1,
        grid=(b, NSA_KV_HEADS, t // tq),
        in_specs=[
            pl.BlockSpec((None, tq, gw), lambda bi, g, i, s: (bi, i, COL_QN // NSA_GROUP + g)),
            per_group((ncp, HEAD_DIM)), per_group((HEAD_DIM, ncp)),
            per_group((t, 2 * HEAD_DIM)), per_group((HEAD_DIM, t)),
            pl.BlockSpec((None, t, HEAD_DIM), lambda bi, g, i, s: (bi, 0, COL_KW + g)),
            per_group((HEAD_DIM, t)),
            pl.BlockSpec((None, tq, LANES), lambda bi, g, i, s: (bi, i, g)),
        ],
        out_specs=pl.BlockSpec((None, tq, gw), lambda bi, g, i, s: (bi, i, g)),
        scratch_shapes=[
            pltpu.VMEM((2 * HEAD_DIM, rt), BF16),
            pltpu.VMEM((2, SLC_TK, rt), F32),
            pltpu.VMEM((1, rt), F32),
            pltpu.VMEM((1, rt), F32),
            pltpu.VMEM((HEAD_DIM, rt), F32),
            pltpu.SMEM((LANES,), jnp.int32),
        ],
    )
    return pl.pallas_call(
        functools.partial(_nsa_kernel, tq=tq, n_slc=n_slc),
        grid_spec=grid_spec,
        out_shape=jax.ShapeDtypeStruct((b, t, NSA_W), BF16),
        compiler_params=_params("arbitrary", "arbitrary", "arbitrary"),
        name="nsa",
    )(slopes, proj, k_cmp, v_cmp_t, ksx, vs_t, proj, vw_t, gates)


def _outproj_kernel(x_ref, of_ref, on_ref, wf_ref, wn_ref, o_ref):
    o_ref[...] = x_ref[...] + _dot(of_ref[...], wf_ref[...]) + _dot(on_ref[...], wn_ref[...])


def _outproj(x2, o_fox, o_nsa, w_fox, w_nsa, tm, tn):
    m, d = x2.shape
    return pl.pallas_call(
        _outproj_kernel,
        grid=(m // tm, d // tn),
        in_specs=[
            pl.BlockSpec((tm, tn), lambda i, j: (i, j)),
            pl.BlockSpec((tm, FOX_W), lambda i, j: (i, 0)),
            pl.BlockSpec((tm, NSA_W), lambda i, j: (i, 0)),
            pl.BlockSpec((FOX_W, tn), lambda i, j: (0, j)),
            pl.BlockSpec((NSA_W, tn), lambda i, j: (0, j)),
        ],
        out_specs=pl.BlockSpec((tm, tn), lambda i, j: (i, j)),
        out_shape=jax.ShapeDtypeStruct((m, d), F32),
        compiler_params=_params("arbitrary", "arbitrary"),
        name="outproj",
    )(x2, o_fox, o_nsa, w_fox, w_nsa)


def _ffn_up_kernel(x_ref, g_ref, wg_ref, wu_ref, o_ref, h_ref):
    @pl.when(pl.program_id(1) == 0)
    def _():
        x = x_ref[...]
        ms = jnp.mean(x * x, axis=-1, keepdims=True)
        h_ref[...] = (x * lax.rsqrt(ms + RMS_EPS) * g_ref[...]).astype(BF16)

    h = h_ref[...]
    a = _dot(h, wg_ref[...])
    u = _dot(h, wu_ref[...])
    o_ref[...] = (a / (1.0 + jnp.exp(-a)) * u).astype(o_ref.dtype)


def _ffn_up(x2, g, w_gate, w_up, tm, tn):
    m, d = x2.shape
    n = w_gate.shape[1]
    return pl.pallas_call(
        _ffn_up_kernel,
        grid=(m // tm, n // tn),
        in_specs=[
            pl.BlockSpec((tm, d), lambda i, j: (i, 0)),
            pl.BlockSpec((1, d), lambda i, j: (0, 0)),
            pl.BlockSpec((d, tn), lambda i, j: (0, j)),
            pl.BlockSpec((d, tn), lambda i, j: (0, j)),
        ],
        out_specs=pl.BlockSpec((tm, tn), lambda i, j: (i, j)),
        out_shape=jax.ShapeDtypeStruct((m, n), BF16),
        scratch_shapes=[pltpu.VMEM((tm, d), BF16)],
        compiler_params=_params("arbitrary", "arbitrary"),
        name="ffn_up",
    )(x2, g, w_gate, w_up)


def _ffn_down_kernel(x_ref, a_ref, w_ref, o_ref):
    o_ref[...] = x_ref[...] + _dot(a_ref[...], w_ref[...])


def _ffn_down(x2, act, w_down, tm, tn):
    m, d = x2.shape
    f = act.shape[1]
    return pl.pallas_call(
        _ffn_down_kernel,
        grid=(m // tm, d // tn),
        in_specs=[
            pl.BlockSpec((tm, tn), lambda i, j: (i, j)),
            pl.BlockSpec((tm, f), lambda i, j: (i, 0)),
            pl.BlockSpec((f, tn), lambda i, j: (0, j)),
        ],
        out_specs=pl.BlockSpec((tm, tn), lambda i, j: (i, j)),
        out_shape=jax.ShapeDtypeStruct((m, d), F32),
        compiler_params=_params("arbitrary", "arbitrary"),
        name="ffn_down",
    )(x2, act, w_down)


def _layer_weights(w_in, fox_forget_bias, fox_q_norm, fox_k_norm, nsa_q_norm, slc_k_norm, win_k_norm):
    splits = (FOX_W, FOX_W, FOX_W, N_FOX_HEADS, NSA_W, KV_W, KV_W, KV_W, KV_W, KV_W, KV_W, 3 * N_NSA_HEADS)
    cuts = [0]
    for s in splits:
        cuts.append(cuts[-1] + s)
    part = lambda k: w_in[:, cuts[k]:cuts[k + 1]]
    w_main = jnp.concatenate([part(k) for k in (0, 1, 2, 4, 5, 6, 7, 8, 9, 10)], axis=1).astype(BF16)
    d = w_in.shape[0]
    w_aux = jnp.concatenate(
        [part(3), part(11), jnp.zeros((d, LANES - N_FOX_HEADS - 3 * N_NSA_HEADS), w_in.dtype)], axis=1).astype(BF16)

    scale = HEAD_DIM ** -0.5
    ones = lambda n: jnp.ones((n * HEAD_DIM,), F32)
    rep = lambda gvec, n, s=1.0: jnp.tile(gvec.astype(F32) * s, n)
    gain = jnp.concatenate([
        rep(fox_q_norm, N_FOX_HEADS, scale), rep(fox_k_norm, N_FOX_HEADS), ones(N_FOX_HEADS),
        rep(nsa_q_norm, N_NSA_HEADS, scale), ones(2 * NSA_KV_HEADS),
        rep(slc_k_norm, NSA_KV_HEADS), ones(NSA_KV_HEADS),
        rep(win_k_norm, NSA_KV_HEADS), ones(NSA_KV_HEADS)])[None, :]
    z = lambda n: jnp.zeros((n * HEAD_DIM,), F32)
    flag = jnp.concatenate([
        ones(2 * N_FOX_HEADS), z(N_FOX_HEADS), ones(N_NSA_HEADS), z(2 * NSA_KV_HEADS),
        ones(NSA_KV_HEADS), z(NSA_KV_HEADS), ones(NSA_KV_HEADS), z(NSA_KV_HEADS)])[None, :]
    bias = jnp.concatenate([fox_forget_bias.astype(F32), jnp.zeros((LANES - N_FOX_HEADS,), F32)])[None, :]
    return w_main, w_aux, gain, flag, bias


def _pick(n, prefs):
    for p in prefs:
        if n % p == 0:
            return p
    return n


def kernel(x, attn_norm, w_in, fox_forget_bias, fox_q_norm, fox_k_norm, nsa_q_norm, cmp_k_norm, slc_k_norm,
           win_k_norm, cmp_pos_k, cmp_pos_v, cmp_w_k, cmp_w_v, w_out, ffn_norm, w_gate, w_up, w_down):
    b, t, d = x.shape
    depth = w_in.shape[0]
    m = b * t
    tm = _pick(m, (1024, 512, 256, 128))
    nch = t // CMP_STRIDE
    slopes = jnp.asarray([2.0 ** (-8.0 * (i + 1) / N_NSA_HEADS) for i in range(N_NSA_HEADS)], F32)

    key = jnp.arange(t)[:, None]
    col = jnp.arange(LANES)[None, :]
    slc_cols = jnp.where(col == 0, key % SLC_BLOCK, key // SLC_BLOCK == col).astype(BF16)

    x2 = x.reshape(m, d)
    for l in range(depth):
        w_main, w_aux, gain, flag, bias = _layer_weights(
            w_in[l], fox_forget_bias[l], fox_q_norm[l], fox_k_norm[l], nsa_q_norm[l], slc_k_norm[l], win_k_norm[l])
        proj, aux = _inproj(x2, attn_norm[l][None, :], w_main, w_aux, gain, flag, tm, 512)
        proj = proj.reshape(b, t, MAIN_W)
        cum, gates = _gates(aux.reshape(b, t, LANES), bias, _pick(t, (512, 256, 128)))
        def chunks(col):
            c = proj[:, :, col * HEAD_DIM:(col + NSA_KV_HEADS) * HEAD_DIM]
            c = c.reshape(b, nch, CMP_STRIDE, NSA_KV_HEADS, HEAD_DIM)
            return jnp.transpose(c, (0, 3, 1, 2, 4)).reshape(b, NSA_KV_HEADS, nch, CMP_STRIDE * HEAD_DIM)

        pos = lambda p: jnp.broadcast_to(p.reshape(1, CMP_BLOCK * HEAD_DIM), (8, CMP_BLOCK * HEAD_DIM))
        k_cmp, v_cmp_t = _compress(
            chunks(COL_KC), chunks(COL_VC),
            cmp_w_k[l].reshape(CMP_BLOCK * HEAD_DIM, HEAD_DIM), cmp_w_v[l].reshape(CMP_BLOCK * HEAD_DIM, HEAD_DIM),
            pos(cmp_pos_k[l]), pos(cmp_pos_v[l]), cmp_k_norm[l][None, :].astype(F32))

        heads = lambda col, n: proj[:, :, col * HEAD_DIM:(col + n) * HEAD_DIM].reshape(b, t, n, HEAD_DIM)
        rows_major = lambda a: jnp.transpose(a, (0, 2, 1, 3))
        dims_major = lambda a: jnp.transpose(a, (0, 2, 3, 1))

        cum_cols = jnp.transpose(cum.reshape(b, t, CUM_PIECES, LANES)[:, :, :, :N_FOX_HEADS], (0, 3, 1, 2))
        cum_cols = jnp.concatenate(
            [cum_cols, jnp.zeros((b, N_FOX_HEADS, t, HEAD_DIM - CUM_PIECES), BF16)], axis=-1)
        kx = jnp.concatenate([rows_major(heads(COL_KF, N_FOX_HEADS)), cum_cols], axis=-1)
        o_fox = _fox(proj, kx, dims_major(heads(COL_VF, N_FOX_HEADS)), _pick(t, (FOX_TQ,)), FOX_TK)

        ksx = jnp.concatenate([rows_major(heads(COL_KS, NSA_KV_HEADS)),
                               jnp.broadcast_to(slc_cols, (b, NSA_KV_HEADS, t, LANES))], axis=-1)
        o_nsa = _nsa(slopes, proj, k_cmp, v_cmp_t, ksx, dims_major(heads(COL_VS, NSA_KV_HEADS)),
                     dims_major(heads(COL_VW, NSA_KV_HEADS)), gates, NSA_TQ)

        wo = w_out[l].astype(BF16)
        x2 = _outproj(x2, o_fox.reshape(m, FOX_W), o_nsa.reshape(m, NSA_W), wo[:FOX_W], wo[FOX_W:], tm, 1024)

        act = _ffn_up(x2, ffn_norm[l][None, :], w_gate[l].astype(BF16), w_up[l].astype(BF16), tm, 512)
        x2 = _ffn_down(x2, act, w_down[l].astype(BF16), _pick(m, (512, 256, 128)), 1024)
    return x2.reshape(b, t, d)
```

```python
import functools
import math

import jax
import jax.numpy as jnp
from jax import lax
from jax.experimental import pallas as pl
from jax.experimental.pallas import tpu as pltpu

HEAD_DIM = 128
N_FOX_HEADS = 8
N_NSA_HEADS = 8
NSA_GROUP = 4
NSA_KV_HEADS = N_NSA_HEADS // NSA_GROUP
CMP_BLOCK = 32
CMP_STRIDE = 16
SLC_BLOCK = 64
SLC_TOPK = 16
WINDOW = 512
RMS_EPS = 1e-6
NEG_INF = -1e30
FORCE_SCORE = 1e9

FOX_W = N_FOX_HEADS * HEAD_DIM
NSA_W = N_NSA_HEADS * HEAD_DIM
LANES = 128
VMEM_LIMIT = 56 * 1024 * 1024
CUM_PIECES = 3
LOG2E = math.log2(math.e)

F32 = jnp.float32
BF16 = jnp.bfloat16

COL_KF = 0
COL_KC = COL_KF + N_FOX_HEADS
COL_VC = COL_KC + NSA_KV_HEADS
COL_KS = COL_VC + NSA_KV_HEADS
COL_KW = COL_KS + NSA_KV_HEADS
ROW_HEADS = COL_KW + NSA_KV_HEADS
VT_QF = 0
VT_QN = VT_QF + N_FOX_HEADS
VT_VF = VT_QN + N_NSA_HEADS
VT_VS = VT_VF + N_FOX_HEADS
VT_VW = VT_VS + NSA_KV_HEADS
VT_HEADS = VT_VW + NSA_KV_HEADS


def _params(*sem):
    return pltpu.CompilerParams(dimension_semantics=sem, vmem_limit_bytes=VMEM_LIMIT)


def _dot(a, b):
    return jnp.dot(a, b, preferred_element_type=F32)


def _split_bf16(x, n):
    pieces, rest = [], x
    for _ in range(n):
        piece = rest.astype(BF16)
        pieces.append(piece)
        rest = rest - piece.astype(F32)
    return pieces


def _inproj_kernel(x_ref, g_ref, w_ref, waux_ref, gain_ref, flag_ref, o_ref, vt_ref, aux_ref, h_ref, *, n_row):
    j = pl.program_id(1)

    @pl.when(j == 0)
    def _():
        x = x_ref[...]
        ms = jnp.mean(x * x, axis=-1, keepdims=True)
        h = (x * lax.rsqrt(ms + RMS_EPS) * g_ref[...]).astype(BF16)
        h_ref[...] = h
        aux_ref[...] = _dot(h, waux_ref[...])

    tn = w_ref.shape[1]
    half = tn // 2

    def normed_heads(part):
        acc = _dot(h_ref[...], w_ref[:, part * half:(part + 1) * half])
        for c in range(half // HEAD_DIM):
            sl = slice(part * half + c * HEAD_DIM, part * half + (c + 1) * HEAD_DIM)
            y = acc[:, c * HEAD_DIM:(c + 1) * HEAD_DIM]
            ms = jnp.mean(y * y, axis=-1, keepdims=True)
            f = flag_ref[:, sl]
            inv = f * lax.rsqrt(ms + RMS_EPS) + (1.0 - f)
            yield sl, y * inv * gain_ref[:, sl]

    @pl.when(j < n_row)
    def _():
        for part in range(2):
            for sl, y in normed_heads(part):
                o_ref[:, sl] = y.astype(o_ref.dtype)

    @pl.when(j >= n_row)
    def _():
        for part in range(2):
            for sl, y in normed_heads(part):
                vt_ref[sl, :] = y.T.astype(vt_ref.dtype)


def _inproj(x2, g, w_all, w_aux, gain, flag, tm, tn):
    m, d = x2.shape
    n_row = ROW_HEADS * HEAD_DIM // tn
    n_vt = VT_HEADS * HEAD_DIM // tn
    return pl.pallas_call(
        functools.partial(_inproj_kernel, n_row=n_row),
        grid=(m // tm, n_row + n_vt),
        in_specs=[
            pl.BlockSpec((tm, d), lambda i, j: (i, 0)),
            pl.BlockSpec((1, d), lambda i, j: (0, 0)),
            pl.BlockSpec((d, tn), lambda i, j: (0, j)),
            pl.BlockSpec((d, LANES), lambda i, j: (0, 0)),
            pl.BlockSpec((1, tn), lambda i, j: (0, j)),
            pl.BlockSpec((1, tn), lambda i, j: (0, j)),
        ],
        out_specs=[
            pl.BlockSpec((tm, tn), lambda i, j: (i, jnp.minimum(j, n_row - 1))),
            pl.BlockSpec((tn, tm), lambda i, j: (jnp.maximum(j - n_row, 0), i)),
            pl.BlockSpec((tm, LANES), lambda i, j: (i, 0)),
        ],
        out_shape=[
            jax.ShapeDtypeStruct((m, ROW_HEADS * HEAD_DIM), BF16),
            jax.ShapeDtypeStruct((VT_HEADS * HEAD_DIM, m), BF16),
            jax.ShapeDtypeStruct((m, LANES), F32),
        ],
        scratch_shapes=[pltpu.VMEM((tm, d), BF16)],
        compiler_params=_params("arbitrary", "arbitrary"),
        name="inproj",
    )(x2, g, w_all, w_aux, gain, flag)


def _gates_kernel(a_ref, b_ref, cum_ref, gate_ref, carry_ref):
    @pl.when(pl.program_id(1) == 0)
    def _():
        carry_ref[...] = jnp.zeros_like(carry_ref)

    a = a_ref[...]
    tc = a.shape[0]
    z = a + b_ref[...]
    lf = jnp.minimum(z, 0.0) - jnp.log(1.0 + jnp.exp(-jnp.abs(z)))
    rows = lax.broadcasted_iota(jnp.int32, (tc, tc), 0)
    cols = lax.broadcasted_iota(jnp.int32, (tc, tc), 1)
    tri = jnp.where(cols <= rows, 1.0, 0.0).astype(F32)
    cs = jnp.dot(tri, lf, preferred_element_type=F32, precision=lax.Precision.HIGHEST) + carry_ref[...]
    carry_ref[...] = cs[tc - 1:tc, :]
    pieces = jnp.concatenate(_split_bf16(cs * (-LOG2E), CUM_PIECES), axis=1)
    src = lax.broadcasted_iota(jnp.int32, (CUM_PIECES * LANES, LANES), 0)
    dst = lax.broadcasted_iota(jnp.int32, (CUM_PIECES * LANES, LANES), 1)
    for h in range(N_FOX_HEADS):
        route = jnp.where((src == dst * LANES + h) & (dst < CUM_PIECES), 1.0, 0.0).astype(BF16)
        cum_ref[h] = _dot(pieces, route).astype(cum_ref.dtype)
    sg = 1.0 / (1.0 + jnp.exp(-a))
    for g in range(NSA_KV_HEADS):
        first = N_FOX_HEADS + g * NSA_GROUP * 3
        gate_ref[:, g * LANES:(g + 1) * LANES] = pltpu.roll(sg, shift=LANES - first, axis=1)


def _gates(aux, bias, tc):
    b, t, _ = aux.shape
    return pl.pallas_call(
        _gates_kernel,
        grid=(b, t // tc),
        in_specs=[
            pl.BlockSpec((None, tc, LANES), lambda bi, i: (bi, i, 0)),
            pl.BlockSpec((1, LANES), lambda bi, i: (0, 0)),
        ],
        out_specs=[
            pl.BlockSpec((None, N_FOX_HEADS, tc, LANES), lambda bi, i: (bi, 0, i, 0)),
            pl.BlockSpec((None, tc, NSA_KV_HEADS * LANES), lambda bi, i: (bi, i, 0)),
        ],
        out_shape=[
            jax.ShapeDtypeStruct((b, N_FOX_HEADS, t, LANES), BF16),
            jax.ShapeDtypeStruct((b, t, NSA_KV_HEADS * LANES), F32),
        ],
        scratch_shapes=[pltpu.VMEM((1, LANES), F32)],
        compiler_params=_params("arbitrary", "arbitrary"),
        name="gates",
    )(aux, bias)


def _compress_kernel(kc_ref, vc_ref, wk_ref, wv_ref, pk_ref, pv_ref, gk_ref, ko_ref, vo_ref):
    def comp(c_ref, w_ref, p_ref):
        a = c_ref[...]
        w = w_ref[...]
        half = w.shape[0] // 2
        wb = w.astype(BF16)
        y0 = _dot(a, wb[:half])
        y1 = _dot(a, wb[half:])
        y1 = pltpu.roll(y1, shift=y1.shape[0] - 1, axis=0)
        pw = jnp.dot(p_ref[...], w, preferred_element_type=F32, precision=lax.Precision.HIGHEST)
        return y0 + y1 + pw[0:1, :]

    k = comp(kc_ref, wk_ref, pk_ref)
    ms = jnp.mean(k * k, axis=-1, keepdims=True)
    ko_ref[:, :HEAD_DIM] = (k * lax.rsqrt(ms + RMS_EPS) * gk_ref[...]).astype(BF16)
    nch = k.shape[0]
    end = lax.broadcasted_iota(jnp.int32, (nch, HEAD_DIM), 0) * CMP_STRIDE + (CMP_BLOCK - 1)
    col = lax.broadcasted_iota(jnp.int32, (nch, HEAD_DIM), 1)
    parts = jnp.where(col == 0, end // LANES * LANES, jnp.where(col == 1, end % LANES, 0))
    ko_ref[:, HEAD_DIM:] = parts.astype(F32).astype(BF16)
    vo_ref[...] = comp(vc_ref, wv_ref, pv_ref).T.astype(BF16)


def _compress(kc_ch, vc_ch, wk, wv, pk, pv, gk):
    b, g, nch, cw = kc_ch.shape
    chunk_spec = pl.BlockSpec((None, None, nch, cw), lambda bi, gi: (bi, gi, 0, 0))
    w_spec = pl.BlockSpec(wk.shape, lambda bi, gi: (0, 0))
    p_spec = pl.BlockSpec(pk.shape, lambda bi, gi: (0, 0))
    return pl.pallas_call(
        _compress_kernel,
        grid=(b, g),
        in_specs=[chunk_spec, chunk_spec, w_spec, w_spec, p_spec, p_spec,
                  pl.BlockSpec((1, HEAD_DIM), lambda bi, gi: (0, 0))],
        out_specs=[pl.BlockSpec((None, None, nch, 2 * HEAD_DIM), lambda bi, gi: (bi, gi, 0, 0)),
                   pl.BlockSpec((None, None, HEAD_DIM, nch), lambda bi, gi: (bi, gi, 0, 0))],
        out_shape=[jax.ShapeDtypeStruct((b, g, nch, 2 * HEAD_DIM), BF16),
                   jax.ShapeDtypeStruct((b, g, HEAD_DIM, nch), BF16)],
        compiler_params=_params("arbitrary", "arbitrary"),
        name="compress",
    )(kc_ch, vc_ch, wk, wv, pk, pv, gk)


def _online_update(s, v_t, m_ref, l_ref, acc_ref, exp):
    m_old = m_ref[...]
    m_new = jnp.maximum(m_old, jnp.max(s, axis=0, keepdims=True))
    alpha = exp(m_old - m_new)
    p = exp(s - m_new)
    l_ref[...] = alpha * l_ref[...] + jnp.sum(p, axis=0, keepdims=True)
    acc_ref[...] = alpha * acc_ref[...] + _dot(v_t, p.astype(BF16))
    m_ref[...] = m_new


def _reset(m_ref, l_ref, acc_ref):
    m_ref[...] = jnp.full(m_ref.shape, NEG_INF, F32)
    l_ref[...] = jnp.zeros(l_ref.shape, F32)
    acc_ref[...] = jnp.zeros(acc_ref.shape, F32)


FOX_TQ = 1024
FOX_TK = 256


def _fox_kernel(q_ref, k_ref, kb_ref, vt_ref, o_ref, kx_ref, qt_ref, s_ref, m_ref, l_ref, acc_ref, *, tq, tk):
    i = pl.program_id(2)

    @pl.when(i == 0)
    def _():
        kx_ref[:, :HEAD_DIM] = k_ref[...]
        kx_ref[:, HEAD_DIM:] = kb_ref[...]

    qt_ref[:HEAD_DIM, :] = q_ref[...]
    row = lax.broadcasted_iota(jnp.int32, (HEAD_DIM, tq), 0)
    qt_ref[HEAD_DIM:, :] = jnp.where(row < CUM_PIECES, 1.0, 0.0).astype(BF16)
    _reset(m_ref, l_ref, acc_ref)

    def scores(j, slot):
        k0 = pl.multiple_of(j * tk, tk)
        s_ref[slot] = _dot(kx_ref[pl.ds(k0, tk), :], qt_ref[...])

    def consume(j, slot, diag):
        k0 = pl.multiple_of(j * tk, tk)
        s = s_ref[slot]
        if diag:
            key = k0 + lax.broadcasted_iota(jnp.int32, (tk, tq), 0)
            t = i * tq + lax.broadcasted_iota(jnp.int32, (tk, tq), 1)
            s = jnp.where(key <= t, s, NEG_INF)
        _online_update(s, vt_ref[:, pl.ds(k0, tk)], m_ref, l_ref, acc_ref, jnp.exp2)

    n = tq // tk
    scores(0, 0)

    def group(gi, c):
        for d in range(n):
            scores(gi * n + d + 1, (d + 1) % 2)
            consume(gi * n + d, d % 2, False)
        return c

    lax.fori_loop(0, i, group, 0)
    for d in range(n):
        if d + 1 < n:
            scores(i * n + d + 1, (d + 1) % 2)
        consume(i * n + d, d % 2, True)
    o = acc_ref[...] * (1.0 / l_ref[...])
    for c in range(tq // LANES):
        o_ref[c * LANES:(c + 1) * LANES, :] = o[:, c * LANES:(c + 1) * LANES].T.astype(o_ref.dtype)


def _fox(proj, cum, vt, tq, tk):
    b, t, _ = proj.shape
    assert (tq // tk) % 2 == 0 and t % tq == 0
    return pl.pallas_call(
        functools.partial(_fox_kernel, tq=tq, tk=tk),
        grid=(b, N_FOX_HEADS, t // tq),
        in_specs=[
            pl.BlockSpec((HEAD_DIM, tq), lambda bi, h, i: (VT_QF + h, bi * (t // tq) + i)),
            pl.BlockSpec((None, t, HEAD_DIM), lambda bi, h, i: (bi, 0, COL_KF + h)),
            pl.BlockSpec((None, None, t, LANES), lambda bi, h, i: (bi, h, 0, 0)),
            pl.BlockSpec((HEAD_DIM, t), lambda bi, h, i: (VT_VF + h, bi)),
        ],
        out_specs=pl.BlockSpec((None, tq, HEAD_DIM), lambda bi, h, i: (bi, i, h)),
        out_shape=jax.ShapeDtypeStruct((b, t, FOX_W), BF16),
        scratch_shapes=[
            pltpu.VMEM((t, 2 * HEAD_DIM), BF16),
            pltpu.VMEM((2 * HEAD_DIM, tq), BF16),
            pltpu.VMEM((2, tk, tq), F32),
            pltpu.VMEM((1, tq), F32),
            pltpu.VMEM((1, tq), F32),
            pltpu.VMEM((HEAD_DIM, tq), F32),
        ],
        compiler_params=_params("arbitrary", "arbitrary", "arbitrary"),
        name="fox",
    )(vt, proj, cum, vt)


NSA_TQ = 128
SLC_TK = 256
MASK_BIAS = -1e9


def _nsa_kernel(slopes_ref, q_ref, kc_ref, vct_ref, ks_ref, kw_ref, pos_ref, vst_ref, vwt_ref, gate_ref, o_ref,
                ksx_ref, kwx_ref, qc_ref, qw_ref, qs_ref, s_ref, m_ref, l_ref, acc_ref, tile_ref, *, tq, n_slc):
    g = pl.program_id(1)
    i = pl.program_id(2)
    q0 = i * tq
    R = NSA_GROUP
    rt = R * tq
    ncp = kc_ref.shape[0]
    nsp = LANES
    slopes = [slopes_ref[g * R + r] for r in range(R)]

    @pl.when(i == 0)
    def _():
        ksx_ref[:, :HEAD_DIM] = ks_ref[...]
        ksx_ref[:, HEAD_DIM:] = pos_ref[...]
        kwx_ref[:, :HEAD_DIM] = kw_ref[...]
        kwx_ref[:, HEAD_DIM:] = pos_ref[...]

    col = lax.broadcasted_iota(jnp.int32, (1, rt), 1)
    rid = col // tq
    slope_row = jnp.where(rid == 0, slopes[0], jnp.where(rid == 1, slopes[1], jnp.where(rid == 2, slopes[2], slopes[3])))
    t_row = q0 + col - rid * tq

    for r in range(R):
        qr = q_ref[r * HEAD_DIM:(r + 1) * HEAD_DIM, :]
        for dst in (qc_ref, qw_ref, qs_ref):
            dst[:HEAD_DIM, r * tq:(r + 1) * tq] = qr
    row2 = lax.broadcasted_iota(jnp.int32, (HEAD_DIM, rt), 0)
    qc_ref[HEAD_DIM:, :] = jnp.where(row2 < 2, slope_row, 0.0).astype(BF16)
    pos_aug = jnp.where(row2 == 0, slope_row, slope_row * (SLC_BLOCK * row2).astype(F32))
    qw_ref[HEAD_DIM:, :] = pos_aug.astype(BF16)

    cmp_end = lax.broadcasted_iota(jnp.int32, (ncp, rt), 0) * CMP_STRIDE + (CMP_BLOCK - 1)
    mask_c = cmp_end <= t_row
    s = jnp.where(mask_c, _dot(kc_ref[...], qc_ref[...]), NEG_INF)
    mx = jnp.max(s, axis=0, keepdims=True)
    p = jnp.where(mask_c, jnp.exp(s - mx), 0.0)
    p = p * (1.0 / jnp.maximum(jnp.sum(p, axis=0, keepdims=True), 1e-30))
    o_cmp = _dot(vct_ref[...], p.astype(BF16))
    p_sum = p[:, 0:tq]
    for r in range(1, R):
        p_sum = p_sum + p[:, r * tq:(r + 1) * tq]

    span = WINDOW + tq
    w0 = pl.multiple_of(jnp.maximum(q0 - WINDOW, 0), tq)
    dist_w = t_row - (w0 + lax.broadcasted_iota(jnp.int32, (span, rt), 0))
    mask_w = (dist_w >= 0) & (dist_w < WINDOW)
    s = jnp.where(mask_w, _dot(kwx_ref[pl.ds(w0, span), :], qw_ref[...]), NEG_INF)
    mx = jnp.max(s, axis=0, keepdims=True)
    p = jnp.exp(s - mx)
    o_win = _dot(vwt_ref[:, pl.ds(w0, span)], p.astype(BF16))
    inv_l_win = 1.0 / jnp.sum(p, axis=0, keepdims=True)

    sj = lax.broadcasted_iota(jnp.int32, (nsp, ncp), 0)
    cn = lax.broadcasted_iota(jnp.int32, (nsp, ncp), 1)
    ov = (cn * CMP_STRIDE < (sj + 1) * SLC_BLOCK) & (cn * CMP_STRIDE + CMP_BLOCK > sj * SLC_BLOCK)
    ov = jnp.where(ov & (cn < ncp - 1) & (sj < n_slc), 1.0, 0.0).astype(BF16)
    imp = sum(_dot(ov, piece) for piece in _split_bf16(p_sum, CUM_PIECES))
    blk = lax.broadcasted_iota(jnp.int32, (nsp, tq), 0)
    tpos = q0 + lax.broadcasted_iota(jnp.int32, (nsp, tq), 1)
    cur = tpos // SLC_BLOCK
    forced = (blk == 0) | (blk == cur) | (blk == cur - 1)
    valid = blk * SLC_BLOCK <= tpos
    work = jnp.where(forced, FORCE_SCORE, jnp.where(valid, imp, -FORCE_SCORE))
    for _ in range(min(SLC_TOPK, n_slc)):
        mx = jnp.max(work, axis=0, keepdims=True)
        first = jnp.min(jnp.where(work == mx, blk, nsp), axis=0, keepdims=True)
        work = jnp.where(blk == first, -jnp.inf, work)
    sel = work == -jnp.inf

    for r in range(R):
        cols = slice(r * tq, (r + 1) * tq)
        aug = jnp.where(sel | (blk == 0), pos_aug[:, cols], MASK_BIAS)
        qs_ref[HEAD_DIM:, cols] = aug.astype(BF16)

    bpt = SLC_TK // SLC_BLOCK
    jd = q0 // SLC_TK
    hit = jnp.max(jnp.where(sel, 1.0, 0.0).T, axis=0, keepdims=True)
    shift = 1
    while shift < bpt:
        hit = jnp.maximum(hit, pltpu.roll(hit, shift=nsp - shift, axis=1))
        shift *= 2
    cnt = jnp.int32(0)
    for j in range(n_slc // bpt):
        tile_ref[cnt] = j
        cnt = cnt + ((hit[0, j * bpt] > 0.5) & (j < jd)).astype(jnp.int32)
    tile_ref[cnt] = jd

    _reset(m_ref, l_ref, acc_ref)

    def scores(idx, slot):
        k0 = pl.multiple_of(tile_ref[idx] * SLC_TK, SLC_TK)
        s_ref[slot] = _dot(ksx_ref[pl.ds(k0, SLC_TK), :], qs_ref[...])

    def consume(idx, slot, diag):
        k0 = pl.multiple_of(tile_ref[idx] * SLC_TK, SLC_TK)
        s = s_ref[slot]
        if diag:
            key = k0 + lax.broadcasted_iota(jnp.int32, (SLC_TK, rt), 0)
            s = jnp.where(key <= t_row, s, NEG_INF)
        _online_update(s, vst_ref[:, pl.ds(k0, SLC_TK)], m_ref, l_ref, acc_ref, jnp.exp)

    scores(0, 0)

    def pair(jj, c):
        scores(2 * jj + 1, 1)
        consume(2 * jj, 0, False)
        scores(2 * jj + 2, 0)
        consume(2 * jj + 1, 1, False)
        return c

    lax.fori_loop(0, cnt // 2, pair, 0)

    @pl.when(cnt % 2 == 1)
    def _():
        scores(cnt, 1)
        consume(cnt - 1, 0, False)
        consume(cnt, 1, True)

    @pl.when(cnt % 2 == 0)
    def _():
        consume(cnt, 0, True)

    o_slc = acc_ref[...]
    inv_l_slc = 1.0 / l_ref[...]

    gt = gate_ref[...].T
    for r in range(R):
        cols = slice(r * tq, (r + 1) * tq)
        o = (gt[3 * r:3 * r + 1] * o_cmp[:, cols]
             + (gt[3 * r + 1:3 * r + 2] * inv_l_slc[:, cols]) * o_slc[:, cols]
             + (gt[3 * r + 2:3 * r + 3] * inv_l_win[:, cols]) * o_win[:, cols])
        o_ref[:, r * HEAD_DIM:(r + 1) * HEAD_DIM] = o.T.astype(o_ref.dtype)


def _nsa(slopes, proj, k_cmp, v_cmp_t, pos_cols, vt, gates, tq):
    b, t, _ = proj.shape
    n_slc = t // SLC_BLOCK
    assert n_slc <= LANES and t % SLC_TK == 0 and t >= WINDOW + tq and SLC_TK % tq == 0 and tq == LANES
    ncp = k_cmp.shape[2]
    gw = NSA_GROUP * HEAD_DIM
    rt = NSA_GROUP * tq
    per_group = lambda shape: pl.BlockSpec((None, None) + shape, lambda bi, g, i, s: (bi, g, 0, 0))
    row_head = lambda col: pl.BlockSpec((None, t, HEAD_DIM), lambda bi, g, i, s: (bi, 0, col + g))
    vt_head = lambda hd: pl.BlockSpec((HEAD_DIM, t), lambda bi, g, i, s: (hd + g, bi))
    grid_spec = pltpu.PrefetchScalarGridSpec(
        num_scalar_prefetch=1,
        grid=(b, NSA_KV_HEADS, t // tq),
        in_specs=[
            pl.BlockSpec((gw, tq), lambda bi, g, i, s: (VT_QN // NSA_GROUP + g, bi * (t // tq) + i)),
            per_group((ncp, 2 * HEAD_DIM)), per_group((HEAD_DIM, ncp)),
            row_head(COL_KS), row_head(COL_KW),
            pl.BlockSpec((t, LANES), lambda bi, g, i, s: (0, 0)),
            vt_head(VT_VS), vt_head(VT_VW),
            pl.BlockSpec((None, tq, LANES), lambda bi, g, i, s: (bi, i, g)),
        ],
        out_specs=pl.BlockSpec((None, tq, gw), lambda bi, g, i, s: (bi, i, g)),
        scratch_shapes=[
            pltpu.VMEM((t, 2 * HEAD_DIM), BF16),
            pltpu.VMEM((t, 2 * HEAD_DIM), BF16),
            pltpu.VMEM((2 * HEAD_DIM, rt), BF16),
            pltpu.VMEM((2 * HEAD_DIM, rt), BF16),
            pltpu.VMEM((2 * HEAD_DIM, rt), BF16),
            pltpu.VMEM((2, SLC_TK, rt), F32),
            pltpu.VMEM((1, rt), F32),
            pltpu.VMEM((1, rt), F32),
            pltpu.VMEM((HEAD_DIM, rt), F32),
            pltpu.SMEM((LANES,), jnp.int32),
        ],
    )
    return pl.pallas_call(
        functools.partial(_nsa_kernel, tq=tq, n_slc=n_slc),
        grid_spec=grid_spec,
        out_shape=jax.ShapeDtypeStruct((b, t, NSA_W), BF16),
        compiler_params=_params("arbitrary", "arbitrary", "arbitrary"),
        name="nsa",
    )(slopes, vt, k_cmp, v_cmp_t, proj, proj, pos_cols, vt, vt, gates)


def _outproj_kernel(x_ref, of_ref, on_ref, wf_ref, wn_ref, o_ref):
    o_ref[...] = x_ref[...] + _dot(of_ref[...], wf_ref[...]) + _dot(on_ref[...], wn_ref[...])


def _outproj(x2, o_fox, o_nsa, w_fox, w_nsa, tm, tn):
    m, d = x2.shape
    return pl.pallas_call(
        _outproj_kernel,
        grid=(m // tm, d // tn),
        in_specs=[
            pl.BlockSpec((tm, tn), lambda i, j: (i, j)),
            pl.BlockSpec((tm, FOX_W), lambda i, j: (i, 0)),
            pl.BlockSpec((tm, NSA_W), lambda i, j: (i, 0)),
            pl.BlockSpec((FOX_W, tn), lambda i, j: (0, j)),
            pl.BlockSpec((NSA_W, tn), lambda i, j: (0, j)),
        ],
        out_specs=pl.BlockSpec((tm, tn), lambda i, j: (i, j)),
        out_shape=jax.ShapeDtypeStruct((m, d), F32),
        compiler_params=_params("arbitrary", "arbitrary"),
        name="outproj",
    )(x2, o_fox, o_nsa, w_fox, w_nsa)


def _ffn_up_kernel(x_ref, g_ref, wg_ref, wu_ref, o_ref, h_ref):
    @pl.when(pl.program_id(1) == 0)
    def _():
        x = x_ref[...]
        ms = jnp.mean(x * x, axis=-1, keepdims=True)
        h_ref[...] = (x * lax.rsqrt(ms + RMS_EPS) * g_ref[...]).astype(BF16)

    h = h_ref[...]
    a = _dot(h, wg_ref[...])
    u = _dot(h, wu_ref[...])
    o_ref[...] = (a / (1.0 + jnp.exp(-a)) * u).astype(o_ref.dtype)


def _ffn_up(x2, g, w_gate, w_up, tm, tn):
    m, d = x2.shape
    n = w_gate.shape[1]
    return pl.pallas_call(
        _ffn_up_kernel,
        grid=(m // tm, n // tn),
        in_specs=[
            pl.BlockSpec((tm, d), lambda i, j: (i, 0)),
            pl.BlockSpec((1, d), lambda i, j: (0, 0)),
            pl.BlockSpec((d, tn), lambda i, j: (0, j)),
            pl.BlockSpec((d, tn), lambda i, j: (0, j)),
        ],
        out_specs=pl.BlockSpec((tm, tn), lambda i, j: (i, j)),
        out_shape=jax.ShapeDtypeStruct((m, n), BF16),
        scratch_shapes=[pltpu.VMEM((tm, d), BF16)],
        compiler_params=_params("arbitrary", "arbitrary"),
        name="ffn_up",
    )(x2, g, w_gate, w_up)


def _ffn_down_kernel(x_ref, a_ref, w_ref, o_ref):
    o_ref[...] = x_ref[...] + _dot(a_ref[...], w_ref[...])


def _ffn_down(x2, act, w_down, tm, tn):
    m, d = x2.shape
    f = act.shape[1]
    return pl.pallas_call(
        _ffn_down_kernel,
        grid=(m // tm, d // tn),
        in_specs=[
            pl.BlockSpec((tm, tn), lambda i, j: (i, j)),
            pl.BlockSpec((tm, f), lambda i, j: (i, 0)),
            pl.BlockSpec((f, tn), lambda i, j: (0, j)),
        ],
        out_specs=pl.BlockSpec((tm, tn), lambda i, j: (i, j)),
        out_shape=jax.ShapeDtypeStruct((m, d), F32),
        compiler_params=_params("arbitrary", "arbitrary"),
        name="ffn_down",
    )(x2, act, w_down)


def _layer_weights(w_in, fox_forget_bias, fox_q_norm, fox_k_norm, nsa_q_norm, slc_k_norm, win_k_norm):
    kv_w = NSA_KV_HEADS * HEAD_DIM
    splits = (FOX_W, FOX_W, FOX_W, N_FOX_HEADS, NSA_W, kv_w, kv_w, kv_w, kv_w, kv_w, kv_w, 3 * N_NSA_HEADS)
    cuts = [0]
    for s in splits:
        cuts.append(cuts[-1] + s)
    part = lambda k: w_in[:, cuts[k]:cuts[k + 1]]
    w_all = jnp.concatenate([part(k) for k in (1, 5, 6, 7, 9, 0, 4, 2, 8, 10)], axis=1).astype(BF16)
    d = w_in.shape[0]
    w_aux = jnp.concatenate(
        [part(3), part(11), jnp.zeros((d, LANES - N_FOX_HEADS - 3 * N_NSA_HEADS), w_in.dtype)], axis=1).astype(BF16)

    scale = HEAD_DIM ** -0.5
    ones = lambda n: jnp.ones((n * HEAD_DIM,), F32)
    rep = lambda gvec, n, s=1.0: jnp.tile(gvec.astype(F32) * s, n)
    n_v = N_FOX_HEADS + 2 * NSA_KV_HEADS
    gain = jnp.concatenate([
        rep(fox_k_norm, N_FOX_HEADS), ones(2 * NSA_KV_HEADS), rep(slc_k_norm, NSA_KV_HEADS),
        rep(win_k_norm, NSA_KV_HEADS),
        rep(fox_q_norm, N_FOX_HEADS, scale * LOG2E), rep(nsa_q_norm, N_NSA_HEADS, scale), ones(n_v)])[None, :]
    z = lambda n: jnp.zeros((n * HEAD_DIM,), F32)
    flag = jnp.concatenate([
        ones(N_FOX_HEADS), z(2 * NSA_KV_HEADS), ones(2 * NSA_KV_HEADS),
        ones(N_FOX_HEADS + N_NSA_HEADS), z(n_v)])[None, :]
    bias = jnp.concatenate([fox_forget_bias.astype(F32), jnp.zeros((LANES - N_FOX_HEADS,), F32)])[None, :]
    return w_all, w_aux, gain, flag, bias


def _pick(n, prefs):
    for p in prefs:
        if n % p == 0:
            return p
    return n


def kernel(x, attn_norm, w_in, fox_forget_bias, fox_q_norm, fox_k_norm, nsa_q_norm, cmp_k_norm, slc_k_norm,
           win_k_norm, cmp_pos_k, cmp_pos_v, cmp_w_k, cmp_w_v, w_out, ffn_norm, w_gate, w_up, w_down):
    b, t, d = x.shape
    depth = w_in.shape[0]
    m = b * t
    tm = _pick(m, (1024, 512, 256, 128))
    nch = t // CMP_STRIDE
    slopes = jnp.asarray([2.0 ** (-8.0 * (i + 1) / N_NSA_HEADS) for i in range(N_NSA_HEADS)], F32)

    key = jnp.arange(t)[:, None]
    col = jnp.arange(LANES)[None, :]
    pos_cols = jnp.where(col == 0, key % SLC_BLOCK, key // SLC_BLOCK == col).astype(BF16)

    x2 = x.reshape(m, d)
    for l in range(depth):
        w_all, w_aux, gain, flag, bias = _layer_weights(
            w_in[l], fox_forget_bias[l], fox_q_norm[l], fox_k_norm[l], nsa_q_norm[l], slc_k_norm[l], win_k_norm[l])
        proj, vt, aux = _inproj(x2, attn_norm[l][None, :], w_all, w_aux, gain, flag, tm, 512)
        proj = proj.reshape(b, t, ROW_HEADS * HEAD_DIM)
        cum, gates = _gates(aux.reshape(b, t, LANES), bias, _pick(t, (512, 256, 128)))

        def chunks(col):
            c = proj[:, :, col * HEAD_DIM:(col + NSA_KV_HEADS) * HEAD_DIM]
            c = c.reshape(b, nch, CMP_STRIDE, NSA_KV_HEADS, HEAD_DIM)
            return jnp.transpose(c, (0, 3, 1, 2, 4)).reshape(b, NSA_KV_HEADS, nch, CMP_STRIDE * HEAD_DIM)

        pos = lambda p: jnp.broadcast_to(p.reshape(1, CMP_BLOCK * HEAD_DIM), (8, CMP_BLOCK * HEAD_DIM))
        k_cmp, v_cmp_t = _compress(
            chunks(COL_KC), chunks(COL_VC),
            cmp_w_k[l].reshape(CMP_BLOCK * HEAD_DIM, HEAD_DIM), cmp_w_v[l].reshape(CMP_BLOCK * HEAD_DIM, HEAD_DIM),
            pos(cmp_pos_k[l]), pos(cmp_pos_v[l]), cmp_k_norm[l][None, :].astype(F32))

        o_fox = _fox(proj, cum, vt, _pick(t, (FOX_TQ, 512)), FOX_TK)
        o_nsa = _nsa(slopes, proj, k_cmp, v_cmp_t, pos_cols, vt, gates, NSA_TQ)

        wo = w_out[l].astype(BF16)
        x2 = _outproj(x2, o_fox.reshape(m, FOX_W), o_nsa.reshape(m, NSA_W), wo[:FOX_W], wo[FOX_W:], tm, 1024)

        act = _ffn_up(x2, ffn_norm[l][None, :], w_gate[l].astype(BF16), w_up[l].astype(BF16), tm, 512)
        x2 = _ffn_down(x2, act, w_down[l].astype(BF16), _pick(m, (512, 256, 128)), 1024)
    return x2.reshape(b, t, d)
```

```python
import functools
import math

import jax
import jax.numpy as jnp
from jax import lax
from jax.experimental import pallas as pl
from jax.experimental.pallas import tpu as pltpu

HEAD_DIM = 128
N_FOX_HEADS = 8
N_NSA_HEADS = 8
NSA_GROUP = 4
NSA_KV_HEADS = N_NSA_HEADS // NSA_GROUP
CMP_BLOCK = 32
CMP_STRIDE = 16
CMP_SHIFT = CMP_STRIDE.bit_length() - 1
SLC_BLOCK = 64
SLC_TOPK = 16
WINDOW = 512
RMS_EPS = 1e-6
NEG_INF = -1e30
FORCE_SCORE = 1e9

FOX_W = N_FOX_HEADS * HEAD_DIM
NSA_W = N_NSA_HEADS * HEAD_DIM
LANES = 128
VMEM_LIMIT = 56 * 1024 * 1024
CUM_PIECES = 3
LOG2E = math.log2(math.e)
GATE_CHUNK = 256

F32 = jnp.float32
BF16 = jnp.bfloat16

COL_KF = 0
COL_KC = COL_KF + N_FOX_HEADS
COL_VC = COL_KC + NSA_KV_HEADS
COL_KS = COL_VC + NSA_KV_HEADS
COL_KW = COL_KS + NSA_KV_HEADS
ROW_HEADS = COL_KW + NSA_KV_HEADS
VT_QF = 0
VT_QN = VT_QF + N_FOX_HEADS
VT_VF = VT_QN + N_NSA_HEADS
VT_VS = VT_VF + N_FOX_HEADS
VT_VW = VT_VS + NSA_KV_HEADS
VT_HEADS = VT_VW + NSA_KV_HEADS


def _params(*sem):
    return pltpu.CompilerParams(dimension_semantics=sem, vmem_limit_bytes=VMEM_LIMIT)


def _dot(a, b):
    return jnp.dot(a, b, preferred_element_type=F32)


def _split_bf16(x, n):
    pieces, rest = [], x
    for _ in range(n):
        piece = rest.astype(BF16)
        pieces.append(piece)
        rest = rest - piece.astype(F32)
    return pieces


def _inproj_kernel(x_ref, g_ref, w_ref, waux_ref, gain_ref, flag_ref, o_ref, vt_ref, aux_ref, h_ref, *, n_row):
    j = pl.program_id(1)

    @pl.when(j == 0)
    def _():
        x = x_ref[...]
        ms = jnp.mean(x * x, axis=-1, keepdims=True)
        h = (x * lax.rsqrt(ms + RMS_EPS) * g_ref[...]).astype(BF16)
        h_ref[...] = h
        aux_ref[...] = _dot(h, waux_ref[...])

    tn = w_ref.shape[1]
    half = tn // 2

    def normed_heads(part):
        acc = _dot(h_ref[...], w_ref[:, part * half:(part + 1) * half])
        for c in range(half // HEAD_DIM):
            sl = slice(part * half + c * HEAD_DIM, part * half + (c + 1) * HEAD_DIM)
            y = acc[:, c * HEAD_DIM:(c + 1) * HEAD_DIM]
            ms = jnp.mean(y * y, axis=-1, keepdims=True)
            f = flag_ref[:, sl]
            inv = f * lax.rsqrt(ms + RMS_EPS) + (1.0 - f)
            yield sl, y * inv * gain_ref[:, sl]

    @pl.when(j < n_row)
    def _():
        for part in range(2):
            for sl, y in normed_heads(part):
                o_ref[:, sl] = y.astype(o_ref.dtype)

    @pl.when(j >= n_row)
    def _():
        for part in range(2):
            for sl, y in normed_heads(part):
                vt_ref[sl, :] = y.T.astype(vt_ref.dtype)


def _inproj(x2, g, w_all, w_aux, gain, flag, tm, tn):
    m, d = x2.shape
    n_row = ROW_HEADS * HEAD_DIM // tn
    n_vt = VT_HEADS * HEAD_DIM // tn
    return pl.pallas_call(
        functools.partial(_inproj_kernel, n_row=n_row),
        grid=(m // tm, n_row + n_vt),
        in_specs=[
            pl.BlockSpec((tm, d), lambda i, j: (i, 0)),
            pl.BlockSpec((1, d), lambda i, j: (0, 0)),
            pl.BlockSpec((d, tn), lambda i, j: (0, j)),
            pl.BlockSpec((d, LANES), lambda i, j: (0, 0)),
            pl.BlockSpec((1, tn), lambda i, j: (0, j)),
            pl.BlockSpec((1, tn), lambda i, j: (0, j)),
        ],
        out_specs=[
            pl.BlockSpec((tm, tn), lambda i, j: (i, jnp.minimum(j, n_row - 1))),
            pl.BlockSpec((tn, tm), lambda i, j: (jnp.maximum(j - n_row, 0), i)),
            pl.BlockSpec((tm, LANES), lambda i, j: (i, 0)),
        ],
        out_shape=[
            jax.ShapeDtypeStruct((m, ROW_HEADS * HEAD_DIM), BF16),
            jax.ShapeDtypeStruct((VT_HEADS * HEAD_DIM, m), BF16),
            jax.ShapeDtypeStruct((m, LANES), F32),
        ],
        scratch_shapes=[pltpu.VMEM((tm, d), BF16)],
        compiler_params=_params("arbitrary", "arbitrary"),
        name="inproj",
    )(x2, g, w_all, w_aux, gain, flag)


def _gates_kernel(a_ref, b_ref, cum_ref, gate_ref, carry_ref):
    @pl.when(pl.program_id(1) == 0)
    def _():
        carry_ref[...] = jnp.zeros_like(carry_ref)

    a = a_ref[...]
    tc = a.shape[0]
    z = a + b_ref[...]
    lf = jnp.minimum(z, 0.0) - jnp.log(1.0 + jnp.exp(-jnp.abs(z)))
    rows = lax.broadcasted_iota(jnp.int32, (GATE_CHUNK, GATE_CHUNK), 0)
    cols = lax.broadcasted_iota(jnp.int32, (GATE_CHUNK, GATE_CHUNK), 1)
    tri = jnp.where(cols <= rows, 1.0, 0.0).astype(BF16)
    carry = carry_ref[...]
    chunks = []
    for c in range(tc // GATE_CHUNK):
        part = lf[c * GATE_CHUNK:(c + 1) * GATE_CHUNK]
        cs_c = sum(_dot(tri, piece) for piece in _split_bf16(part, CUM_PIECES)) + carry
        carry = cs_c[GATE_CHUNK - 1:GATE_CHUNK, :]
        chunks.append(cs_c)
    carry_ref[...] = carry
    cs = jnp.concatenate(chunks, axis=0)
    pieces = _split_bf16(cs * (-LOG2E), CUM_PIECES)
    lane = lax.broadcasted_iota(jnp.int32, (tc, LANES), 1)
    packed = jnp.zeros((tc, LANES), BF16)
    for k, piece in enumerate(pieces):
        moved = piece if k == 0 else pltpu.roll(piece.astype(F32), shift=k * N_FOX_HEADS, axis=1).astype(BF16)
        packed = jnp.where((lane >= k * N_FOX_HEADS) & (lane < (k + 1) * N_FOX_HEADS), moved, packed)
    src = lax.broadcasted_iota(jnp.int32, (LANES, N_FOX_HEADS * LANES), 0)
    dst = lax.broadcasted_iota(jnp.int32, (LANES, N_FOX_HEADS * LANES), 1)
    dst_h, dst_k = dst // LANES, dst % LANES
    route = jnp.where((dst_k < CUM_PIECES) & (src == dst_k * N_FOX_HEADS + dst_h), 1.0, 0.0).astype(BF16)
    routed = _dot(packed, route)
    for h in range(N_FOX_HEADS):
        cum_ref[h] = routed[:, h * LANES:(h + 1) * LANES].astype(cum_ref.dtype)
    sg = 1.0 / (1.0 + jnp.exp(-a))
    for g in range(NSA_KV_HEADS):
        first = N_FOX_HEADS + g * NSA_GROUP * 3
        gate_ref[:, g * LANES:(g + 1) * LANES] = pltpu.roll(sg, shift=LANES - first, axis=1)


def _gates(aux, bias, tc):
    b, t, _ = aux.shape
    return pl.pallas_call(
        _gates_kernel,
        grid=(b, t // tc),
        in_specs=[
            pl.BlockSpec((None, tc, LANES), lambda bi, i: (bi, i, 0)),
            pl.BlockSpec((1, LANES), lambda bi, i: (0, 0)),
        ],
        out_specs=[
            pl.BlockSpec((None, N_FOX_HEADS, tc, LANES), lambda bi, i: (bi, 0, i, 0)),
            pl.BlockSpec((None, tc, NSA_KV_HEADS * LANES), lambda bi, i: (bi, i, 0)),
        ],
        out_shape=[
            jax.ShapeDtypeStruct((b, N_FOX_HEADS, t, LANES), BF16),
            jax.ShapeDtypeStruct((b, t, NSA_KV_HEADS * LANES), F32),
        ],
        scratch_shapes=[pltpu.VMEM((1, LANES), F32)],
        compiler_params=_params("arbitrary", "arbitrary"),
        name="gates",
    )(aux, bias)


def _compress_kernel(kc_ref, vc_ref, wk_ref, wv_ref, pk_ref, pv_ref, gk_ref, ko_ref, vo_ref):
    def comp(c_ref, w_ref, p_ref):
        a = c_ref[...]
        w = w_ref[...]
        half = w.shape[0] // 2
        wb = w.astype(BF16)
        y0 = _dot(a, wb[:half])
        y1 = _dot(a, wb[half:])
        y1 = pltpu.roll(y1, shift=y1.shape[0] - 1, axis=0)
        pw = jnp.dot(p_ref[...], w, preferred_element_type=F32, precision=lax.Precision.HIGHEST)
        return y0 + y1 + pw[0:1, :]

    k = comp(kc_ref, wk_ref, pk_ref)
    ms = jnp.mean(k * k, axis=-1, keepdims=True)
    ko_ref[:, :HEAD_DIM] = (k * lax.rsqrt(ms + RMS_EPS) * gk_ref[...]).astype(BF16)
    nch = k.shape[0]
    end = lax.broadcasted_iota(jnp.int32, (nch, HEAD_DIM), 0) * CMP_STRIDE + (CMP_BLOCK - 1)
    col = lax.broadcasted_iota(jnp.int32, (nch, HEAD_DIM), 1)
    parts = jnp.where(col == 0, end // LANES * LANES, jnp.where(col == 1, end % LANES, 0))
    ko_ref[:, HEAD_DIM:] = parts.astype(F32).astype(BF16)
    vo_ref[...] = comp(vc_ref, wv_ref, pv_ref).T.astype(BF16)


def _compress(kc_ch, vc_ch, wk, wv, pk, pv, gk):
    b, g, nch, cw = kc_ch.shape
    chunk_spec = pl.BlockSpec((None, None, nch, cw), lambda bi, gi: (bi, gi, 0, 0))
    w_spec = pl.BlockSpec(wk.shape, lambda bi, gi: (0, 0))
    p_spec = pl.BlockSpec(pk.shape, lambda bi, gi: (0, 0))
    return pl.pallas_call(
        _compress_kernel,
        grid=(b, g),
        in_specs=[chunk_spec, chunk_spec, w_spec, w_spec, p_spec, p_spec,
                  pl.BlockSpec((1, HEAD_DIM), lambda bi, gi: (0, 0))],
        out_specs=[pl.BlockSpec((None, None, nch, 2 * HEAD_DIM), lambda bi, gi: (bi, gi, 0, 0)),
                   pl.BlockSpec((None, None, HEAD_DIM, nch), lambda bi, gi: (bi, gi, 0, 0))],
        out_shape=[jax.ShapeDtypeStruct((b, g, nch, 2 * HEAD_DIM), BF16),
                   jax.ShapeDtypeStruct((b, g, HEAD_DIM, nch), BF16)],
        compiler_params=_params("arbitrary", "arbitrary"),
        name="compress",
    )(kc_ch, vc_ch, wk, wv, pk, pv, gk)


def _online_update(s, v_t, m_ref, l_ref, acc_ref, exp):
    m_old = m_ref[...]
    m_new = jnp.maximum(m_old, jnp.max(s, axis=0, keepdims=True))
    alpha = exp(m_old - m_new)
    p = exp(s - m_new)
    l_ref[...] = alpha * l_ref[...] + jnp.sum(p, axis=0, keepdims=True)
    acc_ref[...] = alpha * acc_ref[...] + _dot(v_t, p.astype(BF16))
    m_ref[...] = m_new


def _reset(m_ref, l_ref, acc_ref):
    m_ref[...] = jnp.full(m_ref.shape, NEG_INF, F32)
    l_ref[...] = jnp.zeros(l_ref.shape, F32)
    acc_ref[...] = jnp.zeros(acc_ref.shape, F32)


FOX_TQ = 1024
FOX_TK = 256


def _fox_kernel(q_ref, k_ref, kb_ref, vt_ref, o_ref, kx_ref, qt_ref, s_ref, m_ref, l_ref, acc_ref, *, tq, tk):
    i = pl.program_id(2)

    @pl.when(i == 0)
    def _():
        kx_ref[:, :HEAD_DIM] = k_ref[...]
        kx_ref[:, HEAD_DIM:] = kb_ref[...]

    qt_ref[:HEAD_DIM, :] = q_ref[...]
    row = lax.broadcasted_iota(jnp.int32, (HEAD_DIM, tq), 0)
    qt_ref[HEAD_DIM:, :] = jnp.where(row < CUM_PIECES, 1.0, 0.0).astype(BF16)
    _reset(m_ref, l_ref, acc_ref)

    def scores(j, slot, c0=0):
        k0 = pl.multiple_of(j * tk, tk)
        s_ref[slot, :, c0:] = _dot(kx_ref[pl.ds(k0, tk), :], qt_ref[:, c0:])

    def consume(j, slot, diag, c0=0):
        k0 = pl.multiple_of(j * tk, tk)
        s = s_ref[slot, :, c0:]
        if diag:
            key = k0 + lax.broadcasted_iota(jnp.int32, s.shape, 0)
            t = i * tq + c0 + lax.broadcasted_iota(jnp.int32, s.shape, 1)
            s = jnp.where(key <= t, s, NEG_INF)
        _online_update(s, vt_ref[:, pl.ds(k0, tk)], m_ref.at[:, c0:], l_ref.at[:, c0:], acc_ref.at[:, c0:], jnp.exp2)

    n = tq // tk
    scores(0, 0)

    def group(gi, c):
        for d in range(n):
            scores(gi * n + d + 1, (d + 1) % 2)
            consume(gi * n + d, d % 2, False)
        return c

    lax.fori_loop(0, i, group, 0)
    for d in range(n):
        if d + 1 < n:
            scores(i * n + d + 1, (d + 1) % 2, (d + 1) * tk)
        consume(i * n + d, d % 2, True, d * tk)
    o = acc_ref[...] * (1.0 / l_ref[...])
    for c in range(tq // LANES):
        o_ref[c * LANES:(c + 1) * LANES, :] = o[:, c * LANES:(c + 1) * LANES].T.astype(o_ref.dtype)


def _fox(proj, cum, vt, tq, tk):
    b, t, _ = proj.shape
    assert (tq // tk) % 2 == 0 and t % tq == 0
    return pl.pallas_call(
        functools.partial(_fox_kernel, tq=tq, tk=tk),
        grid=(b, N_FOX_HEADS, t // tq),
        in_specs=[
            pl.BlockSpec((HEAD_DIM, tq), lambda bi, h, i: (VT_QF + h, bi * (t // tq) + i)),
            pl.BlockSpec((None, t, HEAD_DIM), lambda bi, h, i: (bi, 0, COL_KF + h)),
            pl.BlockSpec((None, None, t, LANES), lambda bi, h, i: (bi, h, 0, 0)),
            pl.BlockSpec((HEAD_DIM, t), lambda bi, h, i: (VT_VF + h, bi)),
        ],
        out_specs=pl.BlockSpec((None, tq, HEAD_DIM), lambda bi, h, i: (bi, i, h)),
        out_shape=jax.ShapeDtypeStruct((b, t, FOX_W), BF16),
        scratch_shapes=[
            pltpu.VMEM((t, 2 * HEAD_DIM), BF16),
            pltpu.VMEM((2 * HEAD_DIM, tq), BF16),
            pltpu.VMEM((2, tk, tq), F32),
            pltpu.VMEM((1, tq), F32),
            pltpu.VMEM((1, tq), F32),
            pltpu.VMEM((HEAD_DIM, tq), F32),
        ],
        compiler_params=_params("arbitrary", "arbitrary", "arbitrary"),
        name="fox",
    )(vt, proj, cum, vt)


NSA_TQ = 128
SLC_TK = 256
MASK_BIAS = -1e9


def _nsa_kernel(slopes_ref, q_ref, kc_ref, vct_ref, ks_ref, kw_ref, pos_ref, vst_ref, vwt_ref, gate_ref, o_ref,
                ksx_ref, kwx_ref, qc_ref, qw_ref, qs_ref, s_ref, m_ref, l_ref, acc_ref, tile_ref, *, tq, n_slc):
    g = pl.program_id(1)
    i = pl.program_id(2)
    q0 = i * tq
    R = NSA_GROUP
    rt = R * tq
    ncp = kc_ref.shape[0]
    nsp = LANES
    slopes = [slopes_ref[g * R + r] for r in range(R)]

    @pl.when(i == 0)
    def _():
        ksx_ref[:, :HEAD_DIM] = ks_ref[...]
        ksx_ref[:, HEAD_DIM:] = pos_ref[...]
        kwx_ref[:, :HEAD_DIM] = kw_ref[...]
        kwx_ref[:, HEAD_DIM:] = pos_ref[...]

    col = lax.broadcasted_iota(jnp.int32, (1, rt), 1)
    rid = col // tq
    slope_row = jnp.where(rid == 0, slopes[0], jnp.where(rid == 1, slopes[1], jnp.where(rid == 2, slopes[2], slopes[3])))
    t_row = q0 + col - rid * tq

    for r in range(R):
        qr = q_ref[r * HEAD_DIM:(r + 1) * HEAD_DIM, :]
        for dst in (qc_ref, qw_ref, qs_ref):
            dst[:HEAD_DIM, r * tq:(r + 1) * tq] = qr
    row2 = lax.broadcasted_iota(jnp.int32, (HEAD_DIM, rt), 0)
    qc_ref[HEAD_DIM:, :] = jnp.where(row2 < 2, slope_row, 0.0).astype(BF16)
    pos_aug = jnp.where(row2 == 0, slope_row, slope_row * (SLC_BLOCK * row2).astype(F32))
    qw_ref[HEAD_DIM:, :] = pos_aug.astype(BF16)

    last_c = (t_row - (CMP_BLOCK - 1)) >> CMP_SHIFT
    mask_c = lax.broadcasted_iota(jnp.int32, (ncp, rt), 0) <= last_c
    s = jnp.where(mask_c, _dot(kc_ref[...], qc_ref[...]), NEG_INF)
    mx = jnp.max(s, axis=0, keepdims=True)
    p = jnp.exp(s - mx)
    inv_l = jnp.where(last_c >= 0, 1.0 / jnp.maximum(jnp.sum(p, axis=0, keepdims=True), 1e-30), 0.0)
    p = p * inv_l
    o_cmp = _dot(vct_ref[...], p.astype(BF16))
    p_sum = p[:, 0:tq]
    for r in range(1, R):
        p_sum = p_sum + p[:, r * tq:(r + 1) * tq]

    sj = lax.broadcasted_iota(jnp.int32, (nsp, ncp), 0)
    cn = lax.broadcasted_iota(jnp.int32, (nsp, ncp), 1)
    ov = (cn * CMP_STRIDE < (sj + 1) * SLC_BLOCK) & (cn * CMP_STRIDE + CMP_BLOCK > sj * SLC_BLOCK)
    ov = jnp.where(ov & (cn < ncp - 1) & (sj < n_slc), 1.0, 0.0).astype(BF16)
    imp = sum(_dot(ov, piece) for piece in _split_bf16(p_sum, CUM_PIECES))
    blk = lax.broadcasted_iota(jnp.int32, (nsp, tq), 0)
    tpos = q0 + lax.broadcasted_iota(jnp.int32, (nsp, tq), 1)
    cur = tpos // SLC_BLOCK
    forced = (blk == 0) | (blk == cur) | (blk == cur - 1)
    valid = blk * SLC_BLOCK <= tpos
    work = jnp.where(forced, FORCE_SCORE, jnp.where(valid, imp, -FORCE_SCORE))
    for _ in range(min(SLC_TOPK, n_slc)):
        mx = jnp.max(work, axis=0, keepdims=True)
        first = jnp.min(jnp.where(work == mx, blk, nsp), axis=0, keepdims=True)
        work = jnp.where(blk == first, -jnp.inf, work)
    sel = work == -jnp.inf

    for r in range(R):
        cols = slice(r * tq, (r + 1) * tq)
        aug = jnp.where(sel | (blk == 0), pos_aug[:, cols], MASK_BIAS)
        qs_ref[HEAD_DIM:, cols] = aug.astype(BF16)

    span = WINDOW + tq
    w0 = pl.multiple_of(jnp.maximum(q0 - WINDOW, 0), tq)
    rel = lax.broadcasted_iota(jnp.int32, (span, rt), 0)
    mask_w = (rel <= t_row - w0) & (rel > t_row - w0 - WINDOW)
    s = jnp.where(mask_w, _dot(kwx_ref[pl.ds(w0, span), :], qw_ref[...]), NEG_INF)
    mx = jnp.max(s, axis=0, keepdims=True)
    p = jnp.exp(s - mx)
    o_win = _dot(vwt_ref[:, pl.ds(w0, span)], p.astype(BF16))
    inv_l_win = 1.0 / jnp.sum(p, axis=0, keepdims=True)

    gt = gate_ref[...].T
    o_part = []
    for r in range(R):
        cols = slice(r * tq, (r + 1) * tq)
        o_part.append(gt[3 * r:3 * r + 1] * o_cmp[:, cols]
                      + (gt[3 * r + 2:3 * r + 3] * inv_l_win[:, cols]) * o_win[:, cols])

    _reset(m_ref, l_ref, acc_ref)

    def scores(j, slot):
        k0 = pl.multiple_of(j * SLC_TK, SLC_TK)
        s_ref[slot] = _dot(ksx_ref[pl.ds(k0, SLC_TK), :], qs_ref[...])

    def consume(j, slot, diag):
        k0 = pl.multiple_of(j * SLC_TK, SLC_TK)
        s = s_ref[slot]
        if diag:
            key = k0 + lax.broadcasted_iota(jnp.int32, (SLC_TK, rt), 0)
            s = jnp.where(key <= t_row, s, NEG_INF)
        _online_update(s, vst_ref[:, pl.ds(k0, SLC_TK)], m_ref, l_ref, acc_ref, jnp.exp)

    jd = q0 // SLC_TK
    scores(jd, 0)

    bpt = SLC_TK // SLC_BLOCK
    hit = jnp.max(jnp.where(sel, 1.0, 0.0).T, axis=0, keepdims=True)
    shift = 1
    while shift < bpt:
        hit = jnp.maximum(hit, pltpu.roll(hit, shift=nsp - shift, axis=1))
        shift *= 2
    cnt = jnp.int32(0)
    for j in range(n_slc // bpt):
        tile_ref[cnt] = j
        cnt = cnt + ((hit[0, j * bpt] > 0.5) & (j < jd)).astype(jnp.int32)
    tile_ref[cnt] = jd

    scores(tile_ref[0], 1)
    consume(jd, 0, True)

    def pair(jj, c):
        scores(tile_ref[2 * jj + 1], 0)
        consume(tile_ref[2 * jj], 1, False)
        scores(tile_ref[2 * jj + 2], 1)
        consume(tile_ref[2 * jj + 1], 0, False)
        return c

    lax.fori_loop(0, cnt // 2, pair, 0)

    @pl.when(cnt % 2 == 1)
    def _():
        consume(tile_ref[cnt - 1], 1, False)

    inv_l_slc = 1.0 / l_ref[...]
    for r in range(R):
        cols = slice(r * tq, (r + 1) * tq)
        o = o_part[r] + (gt[3 * r + 1:3 * r + 2] * inv_l_slc[:, cols]) * acc_ref[:, cols]
        o_ref[:, r * HEAD_DIM:(r + 1) * HEAD_DIM] = o.T.astype(o_ref.dtype)


def _nsa(slopes, proj, k_cmp, v_cmp_t, pos_cols, vt, gates, tq):
    b, t, _ = proj.shape
    n_slc = t // SLC_BLOCK
    assert n_slc <= LANES and t % SLC_TK == 0 and t >= WINDOW + tq and SLC_TK % tq == 0 and tq == LANES
    ncp = k_cmp.shape[2]
    gw = NSA_GROUP * HEAD_DIM
    rt = NSA_GROUP * tq
    per_group = lambda shape: pl.BlockSpec((None, None) + shape, lambda bi, g, i, s: (bi, g, 0, 0))
    row_head = lambda col: pl.BlockSpec((None, t, HEAD_DIM), lambda bi, g, i, s: (bi, 0, col + g))
    vt_head = lambda hd: pl.BlockSpec((HEAD_DIM, t), lambda bi, g, i, s: (hd + g, bi))
    grid_spec = pltpu.PrefetchScalarGridSpec(
        num_scalar_prefetch=1,
        grid=(b, NSA_KV_HEADS, t // tq),
        in_specs=[
            pl.BlockSpec((gw, tq), lambda bi, g, i, s: (VT_QN // NSA_GROUP + g, bi * (t // tq) + i)),
            per_group((ncp, 2 * HEAD_DIM)), per_group((HEAD_DIM, ncp)),
            row_head(COL_KS), row_head(COL_KW),
            pl.BlockSpec((t, LANES), lambda bi, g, i, s: (0, 0)),
            vt_head(VT_VS), vt_head(VT_VW),
            pl.BlockSpec((None, tq, LANES), lambda bi, g, i, s: (bi, i, g)),
        ],
        out_specs=pl.BlockSpec((None, tq, gw), lambda bi, g, i, s: (bi, i, g)),
        scratch_shapes=[
            pltpu.VMEM((t, 2 * HEAD_DIM), BF16),
            pltpu.VMEM((t, 2 * HEAD_DIM), BF16),
            pltpu.VMEM((2 * HEAD_DIM, rt), BF16),
            pltpu.VMEM((2 * HEAD_DIM, rt), BF16),
            pltpu.VMEM((2 * HEAD_DIM, rt), BF16),
            pltpu.VMEM((2, SLC_TK, rt), F32),
            pltpu.VMEM((1, rt), F32),
            pltpu.VMEM((1, rt), F32),
            pltpu.VMEM((HEAD_DIM, rt), F32),
            pltpu.SMEM((LANES,), jnp.int32),
        ],
    )
    return pl.pallas_call(
        functools.partial(_nsa_kernel, tq=tq, n_slc=n_slc),
        grid_spec=grid_spec,
        out_shape=jax.ShapeDtypeStruct((b, t, NSA_W), BF16),
        compiler_params=_params("arbitrary", "arbitrary", "arbitrary"),
        name="nsa",
    )(slopes, vt, k_cmp, v_cmp_t, proj, proj, pos_cols, vt, vt, gates)


def _outproj_kernel(x_ref, of_ref, on_ref, wf_ref, wn_ref, o_ref):
    o_ref[...] = x_ref[...] + _dot(of_ref[...], wf_ref[...]) + _dot(on_ref[...], wn_ref[...])


def _outproj(x2, o_fox, o_nsa, w_fox, w_nsa, tm, tn):
    m, d = x2.shape
    return pl.pallas_call(
        _outproj_kernel,
        grid=(m // tm, d // tn),
        in_specs=[
            pl.BlockSpec((tm, tn), lambda i, j: (i, j)),
            pl.BlockSpec((tm, FOX_W), lambda i, j: (i, 0)),
            pl.BlockSpec((tm, NSA_W), lambda i, j: (i, 0)),
            pl.BlockSpec((FOX_W, tn), lambda i, j: (0, j)),
            pl.BlockSpec((NSA_W, tn), lambda i, j: (0, j)),
        ],
        out_specs=pl.BlockSpec((tm, tn), lambda i, j: (i, j)),
        out_shape=jax.ShapeDtypeStruct((m, d), F32),
        compiler_params=_params("arbitrary", "arbitrary"),
        name="outproj",
    )(x2, o_fox, o_nsa, w_fox, w_nsa)


def _ffn_up_kernel(x_ref, g_ref, wg_ref, wu_ref, o_ref, h_ref):
    @pl.when(pl.program_id(1) == 0)
    def _():
        x = x_ref[...]
        ms = jnp.mean(x * x, axis=-1, keepdims=True)
        h_ref[...] = (x * lax.rsqrt(ms + RMS_EPS) * g_ref[...]).astype(BF16)

    h = h_ref[...]
    a = _dot(h, wg_ref[...])
    u = _dot(h, wu_ref[...])
    o_ref[...] = (a / (1.0 + jnp.exp(-a)) * u).astype(o_ref.dtype)


def _ffn_up(x2, g, w_gate, w_up, tm, tn):
    m, d = x2.shape
    n = w_gate.shape[1]
    return pl.pallas_call(
        _ffn_up_kernel,
        grid=(m // tm, n // tn),
        in_specs=[
            pl.BlockSpec((tm, d), lambda i, j: (i, 0)),
            pl.BlockSpec((1, d), lambda i, j: (0, 0)),
            pl.BlockSpec((d, tn), lambda i, j: (0, j)),
            pl.BlockSpec((d, tn), lambda i, j: (0, j)),
        ],
        out_specs=pl.BlockSpec((tm, tn), lambda i, j: (i, j)),
        out_shape=jax.ShapeDtypeStruct((m, n), BF16),
        scratch_shapes=[pltpu.VMEM((tm, d), BF16)],
        compiler_params=_params("arbitrary", "arbitrary"),
        name="ffn_up",
    )(x2, g, w_gate, w_up)


def _ffn_down_kernel(x_ref, a_ref, w_ref, o_ref):
    o_ref[...] = x_ref[...] + _dot(a_ref[...], w_ref[...])


def _ffn_down(x2, act, w_down, tm, tn):
    m, d = x2.shape
    f = act.shape[1]
    return pl.pallas_call(
        _ffn_down_kernel,
        grid=(m // tm, d // tn),
        in_specs=[
            pl.BlockSpec((tm, tn), lambda i, j: (i, j)),
            pl.BlockSpec((tm, f), lambda i, j: (i, 0)),
            pl.BlockSpec((f, tn), lambda i, j: (0, j)),
        ],
        out_specs=pl.BlockSpec((tm, tn), lambda i, j: (i, j)),
        out_shape=jax.ShapeDtypeStruct((m, d), F32),
        compiler_params=_params("arbitrary", "arbitrary"),
        name="ffn_down",
    )(x2, act, w_down)


def _layer_weights(w_in, fox_forget_bias, fox_q_norm, fox_k_norm, nsa_q_norm, slc_k_norm, win_k_norm):
    kv_w = NSA_KV_HEADS * HEAD_DIM
    splits = (FOX_W, FOX_W, FOX_W, N_FOX_HEADS, NSA_W, kv_w, kv_w, kv_w, kv_w, kv_w, kv_w, 3 * N_NSA_HEADS)
    cuts = [0]
    for s in splits:
        cuts.append(cuts[-1] + s)
    part = lambda k: w_in[:, cuts[k]:cuts[k + 1]]
    w_all = jnp.concatenate([part(k) for k in (1, 5, 6, 7, 9, 0, 4, 2, 8, 10)], axis=1).astype(BF16)
    d = w_in.shape[0]
    w_aux = jnp.concatenate(
        [part(3), part(11), jnp.zeros((d, LANES - N_FOX_HEADS - 3 * N_NSA_HEADS), w_in.dtype)], axis=1).astype(BF16)

    scale = HEAD_DIM ** -0.5
    ones = lambda n: jnp.ones((n * HEAD_DIM,), F32)
    rep = lambda gvec, n, s=1.0: jnp.tile(gvec.astype(F32) * s, n)
    n_v = N_FOX_HEADS + 2 * NSA_KV_HEADS
    gain = jnp.concatenate([
        rep(fox_k_norm, N_FOX_HEADS), ones(2 * NSA_KV_HEADS), rep(slc_k_norm, NSA_KV_HEADS),
        rep(win_k_norm, NSA_KV_HEADS),
        rep(fox_q_norm, N_FOX_HEADS, scale * LOG2E), rep(nsa_q_norm, N_NSA_HEADS, scale), ones(n_v)])[None, :]
    z = lambda n: jnp.zeros((n * HEAD_DIM,), F32)
    flag = jnp.concatenate([
        ones(N_FOX_HEADS), z(2 * NSA_KV_HEADS), ones(2 * NSA_KV_HEADS),
        ones(N_FOX_HEADS + N_NSA_HEADS), z(n_v)])[None, :]
    bias = jnp.concatenate([fox_forget_bias.astype(F32), jnp.zeros((LANES - N_FOX_HEADS,), F32)])[None, :]
    return w_all, w_aux, gain, flag, bias


def _pick(n, prefs):
    for p in prefs:
        if n % p == 0:
            return p
    return n


def kernel(x, attn_norm, w_in, fox_forget_bias, fox_q_norm, fox_k_norm, nsa_q_norm, cmp_k_norm, slc_k_norm,
           win_k_norm, cmp_pos_k, cmp_pos_v, cmp_w_k, cmp_w_v, w_out, ffn_norm, w_gate, w_up, w_down):
    b, t, d = x.shape
    depth = w_in.shape[0]
    m = b * t
    tm = _pick(m, (1024, 512, 256, 128))
    nch = t // CMP_STRIDE
    slopes = jnp.asarray([2.0 ** (-8.0 * (i + 1) / N_NSA_HEADS) for i in range(N_NSA_HEADS)], F32)

    key = jnp.arange(t)[:, None]
    col = jnp.arange(LANES)[None, :]
    pos_cols = jnp.where(col == 0, key % SLC_BLOCK, key // SLC_BLOCK == col).astype(BF16)

    x2 = x.reshape(m, d)
    for l in range(depth):
        w_all, w_aux, gain, flag, bias = _layer_weights(
            w_in[l], fox_forget_bias[l], fox_q_norm[l], fox_k_norm[l], nsa_q_norm[l], slc_k_norm[l], win_k_norm[l])
        proj, vt, aux = _inproj(x2, attn_norm[l][None, :], w_all, w_aux, gain, flag, tm, 512)
        proj = proj.reshape(b, t, ROW_HEADS * HEAD_DIM)
        cum, gates = _gates(aux.reshape(b, t, LANES), bias, _pick(t, (512, 256, 128)))

        def chunks(col):
            c = proj[:, :, col * HEAD_DIM:(col + NSA_KV_HEADS) * HEAD_DIM]
            c = c.reshape(b, nch, CMP_STRIDE, NSA_KV_HEADS, HEAD_DIM)
            return jnp.transpose(c, (0, 3, 1, 2, 4)).reshape(b, NSA_KV_HEADS, nch, CMP_STRIDE * HEAD_DIM)

        pos = lambda p: jnp.broadcast_to(p.reshape(1, CMP_BLOCK * HEAD_DIM), (8, CMP_BLOCK * HEAD_DIM))
        k_cmp, v_cmp_t = _compress(
            chunks(COL_KC), chunks(COL_VC),
            cmp_w_k[l].reshape(CMP_BLOCK * HEAD_DIM, HEAD_DIM), cmp_w_v[l].reshape(CMP_BLOCK * HEAD_DIM, HEAD_DIM),
            pos(cmp_pos_k[l]), pos(cmp_pos_v[l]), cmp_k_norm[l][None, :].astype(F32))

        o_fox = _fox(proj, cum, vt, _pick(t, (FOX_TQ, 512)), FOX_TK)
        o_nsa = _nsa(slopes, proj, k_cmp, v_cmp_t, pos_cols, vt, gates, NSA_TQ)

        wo = w_out[l].astype(BF16)
        x2 = _outproj(x2, o_fox.reshape(m, FOX_W), o_nsa.reshape(m, NSA_W), wo[:FOX_W], wo[FOX_W:], tm, 1024)

        act = _ffn_up(x2, ffn_norm[l][None, :], w_gate[l].astype(BF16), w_up[l].astype(BF16), tm, 512)
        x2 = _ffn_down(x2, act, w_down[l].astype(BF16), _pick(m, (512, 256, 128)), 1024)
    return x2.reshape(b, t, d)
```

```python
import functools
import math

import jax
import jax.numpy as jnp
from jax import lax
from jax.experimental import pallas as pl
from jax.experimental.pallas import tpu as pltpu

HEAD_DIM = 128
N_FOX_HEADS = 8
N_NSA_HEADS = 8
NSA_GROUP = 4
NSA_KV_HEADS = N_NSA_HEADS // NSA_GROUP
CMP_BLOCK = 32
CMP_STRIDE = 16
CMP_SHIFT = CMP_STRIDE.bit_length() - 1
SLC_BLOCK = 64
SLC_TOPK = 16
WINDOW = 512
RMS_EPS = 1e-6
NEG_INF = -1e30
FORCE_SCORE = 1e9

FOX_W = N_FOX_HEADS * HEAD_DIM
NSA_W = N_NSA_HEADS * HEAD_DIM
LANES = 128
VMEM_LIMIT = 56 * 1024 * 1024
CUM_PIECES = 3
LOG2E = math.log2(math.e)
GATE_CHUNK = 256

F32 = jnp.float32
BF16 = jnp.bfloat16

COL_KF = 0
COL_KC = COL_KF + N_FOX_HEADS
COL_VC = COL_KC + NSA_KV_HEADS
COL_KS = COL_VC + NSA_KV_HEADS
COL_KW = COL_KS + NSA_KV_HEADS
ROW_HEADS = COL_KW + NSA_KV_HEADS
VT_QF = 0
VT_QN = VT_QF + N_FOX_HEADS
VT_VF = VT_QN + N_NSA_HEADS
VT_VS = VT_VF + N_FOX_HEADS
VT_VW = VT_VS + NSA_KV_HEADS
VT_HEADS = VT_VW + NSA_KV_HEADS


def _params(*sem):
    return pltpu.CompilerParams(dimension_semantics=sem, vmem_limit_bytes=VMEM_LIMIT)


def _dot(a, b):
    return jnp.dot(a, b, preferred_element_type=F32)


def _split_bf16(x, n):
    pieces, rest = [], x
    for _ in range(n):
        piece = rest.astype(BF16)
        pieces.append(piece)
        rest = rest - piece.astype(F32)
    return pieces


def _inproj_kernel(x_ref, g_ref, w_ref, waux_ref, gain_ref, flag_ref, o_ref, vt_ref, aux_ref, h_ref, acc_ref,
                   *, n_row, n_tiles):
    j = pl.program_id(1)

    @pl.when(j == 0)
    def _():
        x = x_ref[...]
        ms = jnp.mean(x * x, axis=-1, keepdims=True)
        h = (x * lax.rsqrt(ms + RMS_EPS) * g_ref[...]).astype(BF16)
        h_ref[...] = h
        aux_ref[...] = _dot(h, waux_ref[...])

    def matmul(slot):
        acc_ref[slot] = _dot(h_ref[...], w_ref[...])

    def finish(slot, transposed):
        for c in range(acc_ref.shape[2] // HEAD_DIM):
            sl = slice(c * HEAD_DIM, (c + 1) * HEAD_DIM)
            y = acc_ref[slot, :, sl]
            ms = jnp.mean(y * y, axis=-1, keepdims=True)
            f = flag_ref[:, sl]
            inv = f * lax.rsqrt(ms + RMS_EPS) + (1.0 - f)
            y = y * inv * gain_ref[:, sl]
            if transposed:
                vt_ref[sl, :] = y.T.astype(vt_ref.dtype)
            else:
                o_ref[:, sl] = y.astype(o_ref.dtype)

    @pl.when(j == 0)
    def _():
        matmul(0)

    for parity in (0, 1):
        for transposed in (False, True):
            prev_is_t = j > n_row
            cond = (j >= 1) & (j < n_tiles) & (j % 2 == parity) & (prev_is_t == transposed)

            @pl.when(cond)
            def _(parity=parity, transposed=transposed):
                matmul(parity)
                finish(1 - parity, transposed)

    @pl.when(j == n_tiles)
    def _():
        finish((n_tiles - 1) % 2, True)


def _inproj(x2, g, w_all, w_aux, gain, flag, tm, tn):
    m, d = x2.shape
    n_row = ROW_HEADS * HEAD_DIM // tn
    n_tiles = n_row + VT_HEADS * HEAD_DIM // tn
    assert n_tiles > n_row >= 1
    prev = lambda j: jnp.clip(j - 1, 0, n_tiles - 1)
    return pl.pallas_call(
        functools.partial(_inproj_kernel, n_row=n_row, n_tiles=n_tiles),
        grid=(m // tm, n_tiles + 1),
        in_specs=[
            pl.BlockSpec((tm, d), lambda i, j: (i, 0)),
            pl.BlockSpec((1, d), lambda i, j: (0, 0)),
            pl.BlockSpec((d, tn), lambda i, j: (0, jnp.minimum(j, n_tiles - 1))),
            pl.BlockSpec((d, LANES), lambda i, j: (0, 0)),
            pl.BlockSpec((1, tn), lambda i, j: (0, prev(j))),
            pl.BlockSpec((1, tn), lambda i, j: (0, prev(j))),
        ],
        out_specs=[
            pl.BlockSpec((tm, tn), lambda i, j: (i, jnp.minimum(prev(j), n_row - 1))),
            pl.BlockSpec((tn, tm), lambda i, j: (jnp.maximum(prev(j) - n_row, 0), i)),
            pl.BlockSpec((tm, LANES), lambda i, j: (i, 0)),
        ],
        out_shape=[
            jax.ShapeDtypeStruct((m, ROW_HEADS * HEAD_DIM), BF16),
            jax.ShapeDtypeStruct((VT_HEADS * HEAD_DIM, m), BF16),
            jax.ShapeDtypeStruct((m, LANES), F32),
        ],
        scratch_shapes=[pltpu.VMEM((tm, d), BF16), pltpu.VMEM((2, tm, tn), F32)],
        compiler_params=_params("arbitrary", "arbitrary"),
        name="inproj",
    )(x2, g, w_all, w_aux, gain, flag)


def _gates_kernel(a_ref, b_ref, cum_ref, gate_ref, carry_ref):
    @pl.when(pl.program_id(1) == 0)
    def _():
        carry_ref[...] = jnp.zeros_like(carry_ref)

    a = a_ref[...]
    tc = a.shape[0]
    z = a + b_ref[...]
    lf = jnp.minimum(z, 0.0) - jnp.log(1.0 + jnp.exp(-jnp.abs(z)))
    rows = lax.broadcasted_iota(jnp.int32, (GATE_CHUNK, GATE_CHUNK), 0)
    cols = lax.broadcasted_iota(jnp.int32, (GATE_CHUNK, GATE_CHUNK), 1)
    tri = jnp.where(cols <= rows, 1.0, 0.0).astype(BF16)
    carry = carry_ref[...]
    chunks = []
    for c in range(tc // GATE_CHUNK):
        part = lf[c * GATE_CHUNK:(c + 1) * GATE_CHUNK]
        cs_c = sum(_dot(tri, piece) for piece in _split_bf16(part, CUM_PIECES)) + carry
        carry = cs_c[GATE_CHUNK - 1:GATE_CHUNK, :]
        chunks.append(cs_c)
    carry_ref[...] = carry
    cs = jnp.concatenate(chunks, axis=0)
    pieces = _split_bf16(cs * (-LOG2E), CUM_PIECES)
    lane = lax.broadcasted_iota(jnp.int32, (tc, LANES), 1)
    packed = jnp.zeros((tc, LANES), BF16)
    for k, piece in enumerate(pieces):
        moved = piece if k == 0 else pltpu.roll(piece.astype(F32), shift=k * N_FOX_HEADS, axis=1).astype(BF16)
        packed = jnp.where((lane >= k * N_FOX_HEADS) & (lane < (k + 1) * N_FOX_HEADS), moved, packed)
    src = lax.broadcasted_iota(jnp.int32, (LANES, N_FOX_HEADS * LANES), 0)
    dst = lax.broadcasted_iota(jnp.int32, (LANES, N_FOX_HEADS * LANES), 1)
    dst_h, dst_k = dst // LANES, dst % LANES
    route = jnp.where((dst_k < CUM_PIECES) & (src == dst_k * N_FOX_HEADS + dst_h), 1.0, 0.0).astype(BF16)
    routed = _dot(packed, route)
    for h in range(N_FOX_HEADS):
        cum_ref[h] = routed[:, h * LANES:(h + 1) * LANES].astype(cum_ref.dtype)
    sg = 1.0 / (1.0 + jnp.exp(-a))
    for g in range(NSA_KV_HEADS):
        first = N_FOX_HEADS + g * NSA_GROUP * 3
        gate_ref[:, g * LANES:(g + 1) * LANES] = pltpu.roll(sg, shift=LANES - first, axis=1)


def _gates(aux, bias, tc):
    b, t, _ = aux.shape
    return pl.pallas_call(
        _gates_kernel,
        grid=(b, t // tc),
        in_specs=[
            pl.BlockSpec((None, tc, LANES), lambda bi, i: (bi, i, 0)),
            pl.BlockSpec((1, LANES), lambda bi, i: (0, 0)),
        ],
        out_specs=[
            pl.BlockSpec((None, N_FOX_HEADS, tc, LANES), lambda bi, i: (bi, 0, i, 0)),
            pl.BlockSpec((None, tc, NSA_KV_HEADS * LANES), lambda bi, i: (bi, i, 0)),
        ],
        out_shape=[
            jax.ShapeDtypeStruct((b, N_FOX_HEADS, t, LANES), BF16),
            jax.ShapeDtypeStruct((b, t, NSA_KV_HEADS * LANES), F32),
        ],
        scratch_shapes=[pltpu.VMEM((1, LANES), F32)],
        compiler_params=_params("arbitrary", "arbitrary"),
        name="gates",
    )(aux, bias)


def _compress_kernel(kc_ref, vc_ref, wk_ref, wv_ref, pk_ref, pv_ref, gk_ref, ko_ref, vo_ref):
    def comp(c_ref, w_ref, p_ref):
        a = c_ref[...]
        w = w_ref[...]
        half = w.shape[0] // 2
        wb = w.astype(BF16)
        y0 = _dot(a, wb[:half])
        y1 = _dot(a, wb[half:])
        y1 = pltpu.roll(y1, shift=y1.shape[0] - 1, axis=0)
        pw = jnp.dot(p_ref[...], w, preferred_element_type=F32, precision=lax.Precision.HIGHEST)
        return y0 + y1 + pw[0:1, :]

    k = comp(kc_ref, wk_ref, pk_ref)
    ms = jnp.mean(k * k, axis=-1, keepdims=True)
    ko_ref[:, :HEAD_DIM] = (k * lax.rsqrt(ms + RMS_EPS) * gk_ref[...]).astype(BF16)
    nch = k.shape[0]
    end = lax.broadcasted_iota(jnp.int32, (nch, HEAD_DIM), 0) * CMP_STRIDE + (CMP_BLOCK - 1)
    col = lax.broadcasted_iota(jnp.int32, (nch, HEAD_DIM), 1)
    parts = jnp.where(col == 0, end // LANES * LANES, jnp.where(col == 1, end % LANES, 0))
    ko_ref[:, HEAD_DIM:] = parts.astype(F32).astype(BF16)
    vo_ref[...] = comp(vc_ref, wv_ref, pv_ref).T.astype(BF16)


def _compress(kc_ch, vc_ch, wk, wv, pk, pv, gk):
    b, g, nch, cw = kc_ch.shape
    chunk_spec = pl.BlockSpec((None, None, nch, cw), lambda bi, gi: (bi, gi, 0, 0))
    w_spec = pl.BlockSpec(wk.shape, lambda bi, gi: (0, 0))
    p_spec = pl.BlockSpec(pk.shape, lambda bi, gi: (0, 0))
    return pl.pallas_call(
        _compress_kernel,
        grid=(b, g),
        in_specs=[chunk_spec, chunk_spec, w_spec, w_spec, p_spec, p_spec,
                  pl.BlockSpec((1, HEAD_DIM), lambda bi, gi: (0, 0))],
        out_specs=[pl.BlockSpec((None, None, nch, 2 * HEAD_DIM), lambda bi, gi: (bi, gi, 0, 0)),
                   pl.BlockSpec((None, None, HEAD_DIM, nch), lambda bi, gi: (bi, gi, 0, 0))],
        out_shape=[jax.ShapeDtypeStruct((b, g, nch, 2 * HEAD_DIM), BF16),
                   jax.ShapeDtypeStruct((b, g, HEAD_DIM, nch), BF16)],
        compiler_params=_params("arbitrary", "arbitrary"),
        name="compress",
    )(kc_ch, vc_ch, wk, wv, pk, pv, gk)


def _online_update(s, v_t, m_ref, l_ref, acc_ref, exp):
    m_old = m_ref[...]
    m_new = jnp.maximum(m_old, jnp.max(s, axis=0, keepdims=True))
    alpha = exp(m_old - m_new)
    p = exp(s - m_new)
    l_ref[...] = alpha * l_ref[...] + jnp.sum(p, axis=0, keepdims=True)
    acc_ref[...] = alpha * acc_ref[...] + _dot(v_t, p.astype(BF16))
    m_ref[...] = m_new


def _reset(m_ref, l_ref, acc_ref):
    m_ref[...] = jnp.full(m_ref.shape, NEG_INF, F32)
    l_ref[...] = jnp.zeros(l_ref.shape, F32)
    acc_ref[...] = jnp.zeros(acc_ref.shape, F32)


FOX_TQ = 1024
FOX_TK = 256


def _fox_kernel(q_ref, k_ref, kb_ref, vt_ref, o_ref, kx_ref, qt_ref, s_ref, m_ref, l_ref, acc_ref, *, tq, tk):
    i = pl.program_id(2)

    @pl.when(i == 0)
    def _():
        kx_ref[:, :HEAD_DIM] = k_ref[...]
        kx_ref[:, HEAD_DIM:] = kb_ref[...]

    qt_ref[:HEAD_DIM, :] = q_ref[...]
    row = lax.broadcasted_iota(jnp.int32, (HEAD_DIM, tq), 0)
    qt_ref[HEAD_DIM:, :] = jnp.where(row < CUM_PIECES, 1.0, 0.0).astype(BF16)
    _reset(m_ref, l_ref, acc_ref)

    def scores(j, slot, c0=0):
        k0 = pl.multiple_of(j * tk, tk)
        s_ref[slot, :, c0:] = _dot(kx_ref[pl.ds(k0, tk), :], qt_ref[:, c0:])

    def consume(j, slot, diag, c0=0):
        k0 = pl.multiple_of(j * tk, tk)
        s = s_ref[slot, :, c0:]
        if diag:
            key = k0 + lax.broadcasted_iota(jnp.int32, s.shape, 0)
            t = i * tq + c0 + lax.broadcasted_iota(jnp.int32, s.shape, 1)
            s = jnp.where(key <= t, s, NEG_INF)
        _online_update(s, vt_ref[:, pl.ds(k0, tk)], m_ref.at[:, c0:], l_ref.at[:, c0:], acc_ref.at[:, c0:], jnp.exp2)

    n = tq // tk
    scores(0, 0)

    def group(gi, c):
        for d in range(n):
            scores(gi * n + d + 1, (d + 1) % 2)
            consume(gi * n + d, d % 2, False)
        return c

    lax.fori_loop(0, i, group, 0)
    for d in range(n):
        if d + 1 < n:
            scores(i * n + d + 1, (d + 1) % 2, (d + 1) * tk)
        consume(i * n + d, d % 2, True, d * tk)
    o = acc_ref[...] * (1.0 / l_ref[...])
    for c in range(tq // LANES):
        o_ref[c * LANES:(c + 1) * LANES, :] = o[:, c * LANES:(c + 1) * LANES].T.astype(o_ref.dtype)


def _fox(proj, cum, vt, tq, tk):
    b, t, _ = proj.shape
    assert (tq // tk) % 2 == 0 and t % tq == 0
    return pl.pallas_call(
        functools.partial(_fox_kernel, tq=tq, tk=tk),
        grid=(b, N_FOX_HEADS, t // tq),
        in_specs=[
            pl.BlockSpec((HEAD_DIM, tq), lambda bi, h, i: (VT_QF + h, bi * (t // tq) + i)),
            pl.BlockSpec((None, t, HEAD_DIM), lambda bi, h, i: (bi, 0, COL_KF + h)),
            pl.BlockSpec((None, None, t, LANES), lambda bi, h, i: (bi, h, 0, 0)),
            pl.BlockSpec((HEAD_DIM, t), lambda bi, h, i: (VT_VF + h, bi)),
        ],
        out_specs=pl.BlockSpec((None, tq, HEAD_DIM), lambda bi, h, i: (bi, i, h)),
        out_shape=jax.ShapeDtypeStruct((b, t, FOX_W), BF16),
        scratch_shapes=[
            pltpu.VMEM((t, 2 * HEAD_DIM), BF16),
            pltpu.VMEM((2 * HEAD_DIM, tq), BF16),
            pltpu.VMEM((2, tk, tq), F32),
            pltpu.VMEM((1, tq), F32),
            pltpu.VMEM((1, tq), F32),
            pltpu.VMEM((HEAD_DIM, tq), F32),
        ],
        compiler_params=_params("arbitrary", "arbitrary", "arbitrary"),
        name="fox",
    )(vt, proj, cum, vt)


NSA_TQ = 128
SLC_TK = 256
MASK_BIAS = -1e9


def _nsa_kernel(slopes_ref, q_ref, kc_ref, vct_ref, ks_ref, kw_ref, pos_ref, vst_ref, vwt_ref, gate_ref, o_ref,
                ksx_ref, kwx_ref, qc_ref, qw_ref, qs_ref, s_ref, m_ref, l_ref, acc_ref, tile_ref, *, tq, n_slc):
    g = pl.program_id(1)
    i = pl.program_id(2)
    q0 = i * tq
    R = NSA_GROUP
    rt = R * tq
    ncp = kc_ref.shape[0]
    nsp = LANES
    slopes = [slopes_ref[g * R + r] for r in range(R)]

    @pl.when(i == 0)
    def _():
        ksx_ref[:, :HEAD_DIM] = ks_ref[...]
        ksx_ref[:, HEAD_DIM:] = pos_ref[...]
        kwx_ref[:, :HEAD_DIM] = kw_ref[...]
        kwx_ref[:, HEAD_DIM:] = pos_ref[...]

    col = lax.broadcasted_iota(jnp.int32, (1, rt), 1)
    rid = col // tq
    slope_row = jnp.where(rid == 0, slopes[0], jnp.where(rid == 1, slopes[1], jnp.where(rid == 2, slopes[2], slopes[3])))
    t_row = q0 + col - rid * tq

    for r in range(R):
        qr = q_ref[r * HEAD_DIM:(r + 1) * HEAD_DIM, :]
        for dst in (qc_ref, qw_ref, qs_ref):
            dst[:HEAD_DIM, r * tq:(r + 1) * tq] = qr
    row2 = lax.broadcasted_iota(jnp.int32, (HEAD_DIM, rt), 0)
    qc_ref[HEAD_DIM:, :] = jnp.where(row2 < 2, slope_row, 0.0).astype(BF16)
    pos_aug = jnp.where(row2 == 0, slope_row, slope_row * (SLC_BLOCK * row2).astype(F32))
    qw_ref[HEAD_DIM:, :] = pos_aug.astype(BF16)

    last_c = (t_row - (CMP_BLOCK - 1)) >> CMP_SHIFT
    mask_c = lax.broadcasted_iota(jnp.int32, (ncp, rt), 0) <= last_c
    s = jnp.where(mask_c, _dot(kc_ref[...], qc_ref[...]), NEG_INF)
    mx = jnp.max(s, axis=0, keepdims=True)
    p = jnp.exp(s - mx)
    inv_l = jnp.where(last_c >= 0, 1.0 / jnp.maximum(jnp.sum(p, axis=0, keepdims=True), 1e-30), 0.0)
    p = p * inv_l
    o_cmp = _dot(vct_ref[...], p.astype(BF16))
    p_sum = p[:, 0:tq]
    for r in range(1, R):
        p_sum = p_sum + p[:, r * tq:(r + 1) * tq]

    sj = lax.broadcasted_iota(jnp.int32, (nsp, ncp), 0)
    cn = lax.broadcasted_iota(jnp.int32, (nsp, ncp), 1)
    ov = (cn * CMP_STRIDE < (sj + 1) * SLC_BLOCK) & (cn * CMP_STRIDE + CMP_BLOCK > sj * SLC_BLOCK)
    ov = jnp.where(ov & (cn < ncp - 1) & (sj < n_slc), 1.0, 0.0).astype(BF16)
    imp = sum(_dot(ov, piece) for piece in _split_bf16(p_sum, CUM_PIECES))
    blk = lax.broadcasted_iota(jnp.int32, (nsp, tq), 0)
    tpos = q0 + lax.broadcasted_iota(jnp.int32, (nsp, tq), 1)
    cur = tpos // SLC_BLOCK
    forced = (blk == 0) | (blk == cur) | (blk == cur - 1)
    valid = blk * SLC_BLOCK <= tpos
    work = jnp.where(forced, FORCE_SCORE, jnp.where(valid, imp, -FORCE_SCORE))
    for _ in range(min(SLC_TOPK, n_slc)):
        mx = jnp.max(work, axis=0, keepdims=True)
        first = jnp.min(jnp.where(work == mx, blk, nsp), axis=0, keepdims=True)
        work = jnp.where(blk == first, -jnp.inf, work)
    sel = work == -jnp.inf

    for r in range(R):
        cols = slice(r * tq, (r + 1) * tq)
        aug = jnp.where(sel | (blk == 0), pos_aug[:, cols], MASK_BIAS)
        qs_ref[HEAD_DIM:, cols] = aug.astype(BF16)

    span = WINDOW + tq
    w0 = pl.multiple_of(jnp.maximum(q0 - WINDOW, 0), tq)
    rel = lax.broadcasted_iota(jnp.int32, (span, rt), 0)
    mask_w = (rel <= t_row - w0) & (rel > t_row - w0 - WINDOW)
    s = jnp.where(mask_w, _dot(kwx_ref[pl.ds(w0, span), :], qw_ref[...]), NEG_INF)
    mx = jnp.max(s, axis=0, keepdims=True)
    p = jnp.exp(s - mx)
    o_win = _dot(vwt_ref[:, pl.ds(w0, span)], p.astype(BF16))
    inv_l_win = 1.0 / jnp.sum(p, axis=0, keepdims=True)

    gt = gate_ref[...].T
    o_part = []
    for r in range(R):
        cols = slice(r * tq, (r + 1) * tq)
        o_part.append(gt[3 * r:3 * r + 1] * o_cmp[:, cols]
                      + (gt[3 * r + 2:3 * r + 3] * inv_l_win[:, cols]) * o_win[:, cols])

    _reset(m_ref, l_ref, acc_ref)

    def scores(j, slot):
        k0 = pl.multiple_of(j * SLC_TK, SLC_TK)
        s_ref[slot] = _dot(ksx_ref[pl.ds(k0, SLC_TK), :], qs_ref[...])

    def consume(j, slot, diag):
        k0 = pl.multiple_of(j * SLC_TK, SLC_TK)
        s = s_ref[slot]
        if diag:
            key = k0 + lax.broadcasted_iota(jnp.int32, (SLC_TK, rt), 0)
            s = jnp.where(key <= t_row, s, NEG_INF)
        _online_update(s, vst_ref[:, pl.ds(k0, SLC_TK)], m_ref, l_ref, acc_ref, jnp.exp)

    jd = q0 // SLC_TK
    scores(jd, 0)

    bpt = SLC_TK // SLC_BLOCK
    hit = jnp.max(jnp.where(sel, 1.0, 0.0).T, axis=0, keepdims=True)
    shift = 1
    while shift < bpt:
        hit = jnp.maximum(hit, pltpu.roll(hit, shift=nsp - shift, axis=1))
        shift *= 2
    cnt = jnp.int32(0)
    for j in range(n_slc // bpt):
        tile_ref[cnt] = j
        cnt = cnt + ((hit[0, j * bpt] > 0.5) & (j < jd)).astype(jnp.int32)
    tile_ref[cnt] = jd

    scores(tile_ref[0], 1)
    consume(jd, 0, True)

    def pair(jj, c):
        scores(tile_ref[2 * jj + 1], 0)
        consume(tile_ref[2 * jj], 1, False)
        scores(tile_ref[2 * jj + 2], 1)
        consume(tile_ref[2 * jj + 1], 0, False)
        return c

    lax.fori_loop(0, cnt // 2, pair, 0)

    @pl.when(cnt % 2 == 1)
    def _():
        consume(tile_ref[cnt - 1], 1, False)

    inv_l_slc = 1.0 / l_ref[...]
    for r in range(R):
        cols = slice(r * tq, (r + 1) * tq)
        o = o_part[r] + (gt[3 * r + 1:3 * r + 2] * inv_l_slc[:, cols]) * acc_ref[:, cols]
        o_ref[:, r * HEAD_DIM:(r + 1) * HEAD_DIM] = o.T.astype(o_ref.dtype)


def _nsa(slopes, proj, k_cmp, v_cmp_t, pos_cols, vt, gates, tq):
    b, t, _ = proj.shape
    n_slc = t // SLC_BLOCK
    assert n_slc <= LANES and t % SLC_TK == 0 and t >= WINDOW + tq and SLC_TK % tq == 0 and tq == LANES
    ncp = k_cmp.shape[2]
    gw = NSA_GROUP * HEAD_DIM
    rt = NSA_GROUP * tq
    per_group = lambda shape: pl.BlockSpec((None, None) + shape, lambda bi, g, i, s: (bi, g, 0, 0))
    row_head = lambda col: pl.BlockSpec((None, t, HEAD_DIM), lambda bi, g, i, s: (bi, 0, col + g))
    vt_head = lambda hd: pl.BlockSpec((HEAD_DIM, t), lambda bi, g, i, s: (hd + g, bi))
    grid_spec = pltpu.PrefetchScalarGridSpec(
        num_scalar_prefetch=1,
        grid=(b, NSA_KV_HEADS, t // tq),
        in_specs=[
            pl.BlockSpec((gw, tq), lambda bi, g, i, s: (VT_QN // NSA_GROUP + g, bi * (t // tq) + i)),
            per_group((ncp, 2 * HEAD_DIM)), per_group((HEAD_DIM, ncp)),
            row_head(COL_KS), row_head(COL_KW),
            pl.BlockSpec((t, LANES), lambda bi, g, i, s: (0, 0)),
            vt_head(VT_VS), vt_head(VT_VW),
            pl.BlockSpec((None, tq, LANES), lambda bi, g, i, s: (bi, i, g)),
        ],
        out_specs=pl.BlockSpec((None, tq, gw), lambda bi, g, i, s: (bi, i, g)),
        scratch_shapes=[
            pltpu.VMEM((t, 2 * HEAD_DIM), BF16),
            pltpu.VMEM((t, 2 * HEAD_DIM), BF16),
            pltpu.VMEM((2 * HEAD_DIM, rt), BF16),
            pltpu.VMEM((2 * HEAD_DIM, rt), BF16),
            pltpu.VMEM((2 * HEAD_DIM, rt), BF16),
            pltpu.VMEM((2, SLC_TK, rt), F32),
            pltpu.VMEM((1, rt), F32),
            pltpu.VMEM((1, rt), F32),
            pltpu.VMEM((HEAD_DIM, rt), F32),
            pltpu.SMEM((LANES,), jnp.int32),
        ],
    )
    return pl.pallas_call(
        functools.partial(_nsa_kernel, tq=tq, n_slc=n_slc),
        grid_spec=grid_spec,
        out_shape=jax.ShapeDtypeStruct((b, t, NSA_W), BF16),
        compiler_params=_params("arbitrary", "arbitrary", "arbitrary"),
        name="nsa",
    )(slopes, vt, k_cmp, v_cmp_t, proj, proj, pos_cols, vt, vt, gates)


def _outproj_kernel(x_ref, of_ref, on_ref, wf_ref, wn_ref, o_ref):
    o_ref[...] = x_ref[...] + _dot(of_ref[...], wf_ref[...]) + _dot(on_ref[...], wn_ref[...])


def _outproj(x2, o_fox, o_nsa, w_fox, w_nsa, tm, tn):
    m, d = x2.shape
    return pl.pallas_call(
        _outproj_kernel,
        grid=(m // tm, d // tn),
        in_specs=[
            pl.BlockSpec((tm, tn), lambda i, j: (i, j)),
            pl.BlockSpec((tm, FOX_W), lambda i, j: (i, 0)),
            pl.BlockSpec((tm, NSA_W), lambda i, j: (i, 0)),
            pl.BlockSpec((FOX_W, tn), lambda i, j: (0, j)),
            pl.BlockSpec((NSA_W, tn), lambda i, j: (0, j)),
        ],
        out_specs=pl.BlockSpec((tm, tn), lambda i, j: (i, j)),
        out_shape=jax.ShapeDtypeStruct((m, d), F32),
        compiler_params=_params("arbitrary", "arbitrary"),
        name="outproj",
    )(x2, o_fox, o_nsa, w_fox, w_nsa)


def _ffn_up_kernel(x_ref, g_ref, wg_ref, wu_ref, o_ref, h_ref):
    @pl.when(pl.program_id(1) == 0)
    def _():
        x = x_ref[...]
        ms = jnp.mean(x * x, axis=-1, keepdims=True)
        h_ref[...] = (x * lax.rsqrt(ms + RMS_EPS) * g_ref[...]).astype(BF16)

    h = h_ref[...]
    a = _dot(h, wg_ref[...])
    u = _dot(h, wu_ref[...])
    o_ref[...] = (a / (1.0 + jnp.exp(-a)) * u).astype(o_ref.dtype)


def _ffn_up(x2, g, w_gate, w_up, tm, tn):
    m, d = x2.shape
    n = w_gate.shape[1]
    return pl.pallas_call(
        _ffn_up_kernel,
        grid=(m // tm, n // tn),
        in_specs=[
            pl.BlockSpec((tm, d), lambda i, j: (i, 0)),
            pl.BlockSpec((1, d), lambda i, j: (0, 0)),
            pl.BlockSpec((d, tn), lambda i, j: (0, j)),
            pl.BlockSpec((d, tn), lambda i, j: (0, j)),
        ],
        out_specs=pl.BlockSpec((tm, tn), lambda i, j: (i, j)),
        out_shape=jax.ShapeDtypeStruct((m, n), BF16),
        scratch_shapes=[pltpu.VMEM((tm, d), BF16)],
        compiler_params=_params("arbitrary", "arbitrary"),
        name="ffn_up",
    )(x2, g, w_gate, w_up)


def _ffn_down_kernel(x_ref, a_ref, w_ref, o_ref):
    o_ref[...] = x_ref[...] + _dot(a_ref[...], w_ref[...])


def _ffn_down(x2, act, w_down, tm, tn):
    m, d = x2.shape
    f = act.shape[1]
    return pl.pallas_call(
        _ffn_down_kernel,
        grid=(m // tm, d // tn),
        in_specs=[
            pl.BlockSpec((tm, tn), lambda i, j: (i, j)),
            pl.BlockSpec((tm, f), lambda i, j: (i, 0)),
            pl.BlockSpec((f, tn), lambda i, j: (0, j)),
        ],
        out_specs=pl.BlockSpec((tm, tn), lambda i, j: (i, j)),
        out_shape=jax.ShapeDtypeStruct((m, d), F32),
        compiler_params=_params("arbitrary", "arbitrary"),
        name="ffn_down",
    )(x2, act, w_down)


def _layer_weights(w_in, fox_forget_bias, fox_q_norm, fox_k_norm, nsa_q_norm, slc_k_norm, win_k_norm):
    kv_w = NSA_KV_HEADS * HEAD_DIM
    splits = (FOX_W, FOX_W, FOX_W, N_FOX_HEADS, NSA_W, kv_w, kv_w, kv_w, kv_w, kv_w, kv_w, 3 * N_NSA_HEADS)
    cuts = [0]
    for s in splits:
        cuts.append(cuts[-1] + s)
    part = lambda k: w_in[:, cuts[k]:cuts[k + 1]]
    w_all = jnp.concatenate([part(k) for k in (1, 5, 6, 7, 9, 0, 4, 2, 8, 10)], axis=1).astype(BF16)
    d = w_in.shape[0]
    w_aux = jnp.concatenate(
        [part(3), part(11), jnp.zeros((d, LANES - N_FOX_HEADS - 3 * N_NSA_HEADS), w_in.dtype)], axis=1).astype(BF16)

    scale = HEAD_DIM ** -0.5
    ones = lambda n: jnp.ones((n * HEAD_DIM,), F32)
    rep = lambda gvec, n, s=1.0: jnp.tile(gvec.astype(F32) * s, n)
    n_v = N_FOX_HEADS + 2 * NSA_KV_HEADS
    gain = jnp.concatenate([
        rep(fox_k_norm, N_FOX_HEADS), ones(2 * NSA_KV_HEADS), rep(slc_k_norm, NSA_KV_HEADS),
        rep(win_k_norm, NSA_KV_HEADS),
        rep(fox_q_norm, N_FOX_HEADS, scale * LOG2E), rep(nsa_q_norm, N_NSA_HEADS, scale), ones(n_v)])[None, :]
    z = lambda n: jnp.zeros((n * HEAD_DIM,), F32)
    flag = jnp.concatenate([
        ones(N_FOX_HEADS), z(2 * NSA_KV_HEADS), ones(2 * NSA_KV_HEADS),
        ones(N_FOX_HEADS + N_NSA_HEADS), z(n_v)])[None, :]
    bias = jnp.concatenate([fox_forget_bias.astype(F32), jnp.zeros((LANES - N_FOX_HEADS,), F32)])[None, :]
    return w_all, w_aux, gain, flag, bias


def _pick(n, prefs):
    for p in prefs:
        if n % p == 0:
            return p
    return n


def kernel(x, attn_norm, w_in, fox_forget_bias, fox_q_norm, fox_k_norm, nsa_q_norm, cmp_k_norm, slc_k_norm,
           win_k_norm, cmp_pos_k, cmp_pos_v, cmp_w_k, cmp_w_v, w_out, ffn_norm, w_gate, w_up, w_down):
    b, t, d = x.shape
    depth = w_in.shape[0]
    m = b * t
    tm = _pick(m, (1024, 512, 256, 128))
    nch = t // CMP_STRIDE
    slopes = jnp.asarray([2.0 ** (-8.0 * (i + 1) / N_NSA_HEADS) for i in range(N_NSA_HEADS)], F32)

    key = jnp.arange(t)[:, None]
    col = jnp.arange(LANES)[None, :]
    pos_cols = jnp.where(col == 0, key % SLC_BLOCK, key // SLC_BLOCK == col).astype(BF16)

    x2 = x.reshape(m, d)
    for l in range(depth):
        w_all, w_aux, gain, flag, bias = _layer_weights(
            w_in[l], fox_forget_bias[l], fox_q_norm[l], fox_k_norm[l], nsa_q_norm[l], slc_k_norm[l], win_k_norm[l])
        proj, vt, aux = _inproj(x2, attn_norm[l][None, :], w_all, w_aux, gain, flag, tm, 512)
        proj = proj.reshape(b, t, ROW_HEADS * HEAD_DIM)
        cum, gates = _gates(aux.reshape(b, t, LANES), bias, _pick(t, (512, 256, 128)))

        def chunks(col):
            c = proj[:, :, col * HEAD_DIM:(col + NSA_KV_HEADS) * HEAD_DIM]
            c = c.reshape(b, nch, CMP_STRIDE, NSA_KV_HEADS, HEAD_DIM)
            return jnp.transpose(c, (0, 3, 1, 2, 4)).reshape(b, NSA_KV_HEADS, nch, CMP_STRIDE * HEAD_DIM)

        pos = lambda p: jnp.broadcast_to(p.reshape(1, CMP_BLOCK * HEAD_DIM), (8, CMP_BLOCK * HEAD_DIM))
        k_cmp, v_cmp_t = _compress(
            chunks(COL_KC), chunks(COL_VC),
            cmp_w_k[l].reshape(CMP_BLOCK * HEAD_DIM, HEAD_DIM), cmp_w_v[l].reshape(CMP_BLOCK * HEAD_DIM, HEAD_DIM),
            pos(cmp_pos_k[l]), pos(cmp_pos_v[l]), cmp_k_norm[l][None, :].astype(F32))

        o_fox = _fox(proj, cum, vt, _pick(t, (FOX_TQ, 512)), FOX_TK)
        o_nsa = _nsa(slopes, proj, k_cmp, v_cmp_t, pos_cols, vt, gates, NSA_TQ)

        wo = w_out[l].astype(BF16)
        x2 = _outproj(x2, o_fox.reshape(m, FOX_W), o_nsa.reshape(m, NSA_W), wo[:FOX_W], wo[FOX_W:], tm, 1024)

        act = _ffn_up(x2, ffn_norm[l][None, :], w_gate[l].astype(BF16), w_up[l].astype(BF16), tm, 512)
        x2 = _ffn_down(x2, act, w_down[l].astype(BF16), _pick(m, (512, 256, 128)), 1024)
    return x2.reshape(b, t, d)
```

```python
import functools
import math

import jax
import jax.numpy as jnp
from jax import lax
from jax.experimental import pallas as pl
from jax.experimental.pallas import tpu as pltpu

HEAD_DIM = 128
N_FOX_HEADS = 8
N_NSA_HEADS = 8
NSA_GROUP = 4
NSA_KV_HEADS = N_NSA_HEADS // NSA_GROUP
CMP_BLOCK = 32
CMP_STRIDE = 16
CMP_SHIFT = CMP_STRIDE.bit_length() - 1
SLC_BLOCK = 64
SLC_TOPK = 16
WINDOW = 512
RMS_EPS = 1e-6
NEG_INF = -1e30
FORCE_SCORE = 1e9

FOX_W = N_FOX_HEADS * HEAD_DIM
NSA_W = N_NSA_HEADS * HEAD_DIM
LANES = 128
VMEM_LIMIT = 56 * 1024 * 1024
CUM_PIECES = 3
LOG2E = math.log2(math.e)
GATE_CHUNK = 256
ROW_TILE = 1024
PROJ_COLS = 512
OUT_COLS = 1024
FFN_COLS = 512
DOWN_ROW_TILE = 512
GATE_ROWS = 512

F32 = jnp.float32
BF16 = jnp.bfloat16

COL_KF = 0
COL_KS = COL_KF + N_FOX_HEADS
COL_KW = COL_KS + NSA_KV_HEADS
ROW_HEADS = COL_KW + NSA_KV_HEADS
CV_K = 0
CV_V = CV_K + NSA_KV_HEADS
CV_HEADS = CV_V + NSA_KV_HEADS
VT_QF = 0
VT_QN = VT_QF + N_FOX_HEADS
VT_VF = VT_QN + N_NSA_HEADS
VT_VS = VT_VF + N_FOX_HEADS
VT_VW = VT_VS + NSA_KV_HEADS
VT_HEADS = VT_VW + NSA_KV_HEADS


def _params(*sem):
    return pltpu.CompilerParams(dimension_semantics=sem, vmem_limit_bytes=VMEM_LIMIT)


def _dot(a, b):
    return jnp.dot(a, b, preferred_element_type=F32)


def _split_bf16(x, n):
    pieces, rest = [], x
    for _ in range(n):
        piece = rest.astype(BF16)
        pieces.append(piece)
        rest = rest - piece.astype(F32)
    return pieces


def _inproj_kernel(x_ref, g_ref, w_ref, waux_ref, gain_ref, flag_ref, o_ref, cv_ref, vt_ref, aux_ref, h_ref, acc_ref,
                   *, n_row, n_tiles):
    j = pl.program_id(1)

    @pl.when(j == 0)
    def _():
        x = x_ref[...]
        ms = jnp.mean(x * x, axis=-1, keepdims=True)
        h = (x * lax.rsqrt(ms + RMS_EPS) * g_ref[...]).astype(BF16)
        h_ref[...] = h
        aux_ref[...] = _dot(h, waux_ref[...])

    def matmul(slot):
        acc_ref[slot] = _dot(h_ref[...], w_ref[...])

    ROW, HEAD_MAJOR, TRANSPOSED = 0, 1, 2

    def finish(slot, kind):
        for c in range(acc_ref.shape[2] // HEAD_DIM):
            sl = slice(c * HEAD_DIM, (c + 1) * HEAD_DIM)
            y = acc_ref[slot, :, sl]
            ms = jnp.mean(y * y, axis=-1, keepdims=True)
            f = flag_ref[:, sl]
            inv = f * lax.rsqrt(ms + RMS_EPS) + (1.0 - f)
            y = y * inv * gain_ref[:, sl]
            if kind == TRANSPOSED:
                vt_ref[sl, :] = y.T.astype(vt_ref.dtype)
            elif kind == HEAD_MAJOR:
                cv_ref[c] = y.astype(cv_ref.dtype)
            else:
                o_ref[:, sl] = y.astype(o_ref.dtype)

    @pl.when(j == 0)
    def _():
        matmul(0)

    prev_kind = jnp.where(j - 1 < n_row, ROW, jnp.where(j - 1 == n_row, HEAD_MAJOR, TRANSPOSED))
    for parity in (0, 1):
        for kind in (ROW, HEAD_MAJOR, TRANSPOSED):
            cond = (j >= 1) & (j < n_tiles) & (j % 2 == parity) & (prev_kind == kind)

            @pl.when(cond)
            def _(parity=parity, kind=kind):
                matmul(parity)
                finish(1 - parity, kind)

    @pl.when(j == n_tiles)
    def _():
        finish((n_tiles - 1) % 2, TRANSPOSED)


def _inproj(x2, g, w_all, w_aux, gain, flag, tm, tn):
    m, d = x2.shape
    n_row = ROW_HEADS * HEAD_DIM // tn
    assert tn == CV_HEADS * HEAD_DIM
    n_tiles = n_row + 1 + VT_HEADS * HEAD_DIM // tn
    prev = lambda j: jnp.clip(j - 1, 0, n_tiles - 1)
    return pl.pallas_call(
        functools.partial(_inproj_kernel, n_row=n_row, n_tiles=n_tiles),
        grid=(m // tm, n_tiles + 1),
        in_specs=[
            pl.BlockSpec((tm, d), lambda i, j: (i, 0)),
            pl.BlockSpec((1, d), lambda i, j: (0, 0)),
            pl.BlockSpec((d, tn), lambda i, j: (0, jnp.minimum(j, n_tiles - 1))),
            pl.BlockSpec((d, LANES), lambda i, j: (0, 0)),
            pl.BlockSpec((1, tn), lambda i, j: (0, prev(j))),
            pl.BlockSpec((1, tn), lambda i, j: (0, prev(j))),
        ],
        out_specs=[
            pl.BlockSpec((tm, tn), lambda i, j: (i, jnp.minimum(prev(j), n_row - 1))),
            pl.BlockSpec((CV_HEADS, tm, HEAD_DIM), lambda i, j: (0, i, 0)),
            pl.BlockSpec((tn, tm), lambda i, j: (jnp.maximum(prev(j) - n_row - 1, 0), i)),
            pl.BlockSpec((tm, LANES), lambda i, j: (i, 0)),
        ],
        out_shape=[
            jax.ShapeDtypeStruct((m, ROW_HEADS * HEAD_DIM), BF16),
            jax.ShapeDtypeStruct((CV_HEADS, m, HEAD_DIM), BF16),
            jax.ShapeDtypeStruct((VT_HEADS * HEAD_DIM, m), BF16),
            jax.ShapeDtypeStruct((m, LANES), F32),
        ],
        scratch_shapes=[pltpu.VMEM((tm, d), BF16), pltpu.VMEM((2, tm, tn), F32)],
        compiler_params=_params("arbitrary", "arbitrary"),
        name="inproj",
    )(x2, g, w_all, w_aux, gain, flag)


def _gates_kernel(a_ref, b_ref, cum_ref, gate_ref, carry_ref):
    @pl.when(pl.program_id(1) == 0)
    def _():
        carry_ref[...] = jnp.zeros_like(carry_ref)

    a = a_ref[...]
    tc = a.shape[0]
    z = a + b_ref[...]
    lf = jnp.minimum(z, 0.0) - jnp.log(1.0 + jnp.exp(-jnp.abs(z)))
    rows = lax.broadcasted_iota(jnp.int32, (GATE_CHUNK, GATE_CHUNK), 0)
    cols = lax.broadcasted_iota(jnp.int32, (GATE_CHUNK, GATE_CHUNK), 1)
    tri = jnp.where(cols <= rows, 1.0, 0.0).astype(BF16)
    carry = carry_ref[...]
    chunks = []
    for c in range(tc // GATE_CHUNK):
        part = lf[c * GATE_CHUNK:(c + 1) * GATE_CHUNK]
        cs_c = sum(_dot(tri, piece) for piece in _split_bf16(part, CUM_PIECES)) + carry
        carry = cs_c[GATE_CHUNK - 1:GATE_CHUNK, :]
        chunks.append(cs_c)
    carry_ref[...] = carry
    cs = jnp.concatenate(chunks, axis=0)
    pieces = _split_bf16(cs * (-LOG2E), CUM_PIECES)
    lane = lax.broadcasted_iota(jnp.int32, (tc, LANES), 1)
    packed = jnp.zeros((tc, LANES), BF16)
    for k, piece in enumerate(pieces):
        moved = piece if k == 0 else pltpu.roll(piece.astype(F32), shift=k * N_FOX_HEADS, axis=1).astype(BF16)
        packed = jnp.where((lane >= k * N_FOX_HEADS) & (lane < (k + 1) * N_FOX_HEADS), moved, packed)
    src = lax.broadcasted_iota(jnp.int32, (LANES, N_FOX_HEADS * LANES), 0)
    dst = lax.broadcasted_iota(jnp.int32, (LANES, N_FOX_HEADS * LANES), 1)
    dst_h, dst_k = dst // LANES, dst % LANES
    route = jnp.where((dst_k < CUM_PIECES) & (src == dst_k * N_FOX_HEADS + dst_h), 1.0, 0.0).astype(BF16)
    routed = _dot(packed, route)
    for h in range(N_FOX_HEADS):
        cum_ref[h] = routed[:, h * LANES:(h + 1) * LANES].astype(cum_ref.dtype)
    sg = 1.0 / (1.0 + jnp.exp(-a))
    for g in range(NSA_KV_HEADS):
        first = N_FOX_HEADS + g * NSA_GROUP * 3
        gate_ref[:, g * LANES:(g + 1) * LANES] = pltpu.roll(sg, shift=LANES - first, axis=1)


def _gates(aux, bias, tc):
    b, t, _ = aux.shape
    return pl.pallas_call(
        _gates_kernel,
        grid=(b, t // tc),
        in_specs=[
            pl.BlockSpec((None, tc, LANES), lambda bi, i: (bi, i, 0)),
            pl.BlockSpec((1, LANES), lambda bi, i: (0, 0)),
        ],
        out_specs=[
            pl.BlockSpec((None, N_FOX_HEADS, tc, LANES), lambda bi, i: (bi, 0, i, 0)),
            pl.BlockSpec((None, tc, NSA_KV_HEADS * LANES), lambda bi, i: (bi, i, 0)),
        ],
        out_shape=[
            jax.ShapeDtypeStruct((b, N_FOX_HEADS, t, LANES), BF16),
            jax.ShapeDtypeStruct((b, t, NSA_KV_HEADS * LANES), F32),
        ],
        scratch_shapes=[pltpu.VMEM((1, LANES), F32)],
        compiler_params=_params("arbitrary", "arbitrary"),
        name="gates",
    )(aux, bias)


def _compress_kernel(kc_ref, vc_ref, wk_ref, wv_ref, pk_ref, pv_ref, gk_ref, ko_ref, vo_ref):
    def comp(c_ref, w_ref, p_ref):
        a = c_ref[...]
        w = w_ref[...]
        half = w.shape[0] // 2
        wb = w.astype(BF16)
        y0 = _dot(a, wb[:half])
        y1 = _dot(a, wb[half:])
        y1 = pltpu.roll(y1, shift=y1.shape[0] - 1, axis=0)
        pw = jnp.dot(p_ref[...], w, preferred_element_type=F32, precision=lax.Precision.HIGHEST)
        return y0 + y1 + pw[0:1, :]

    k = comp(kc_ref, wk_ref, pk_ref)
    ms = jnp.mean(k * k, axis=-1, keepdims=True)
    ko_ref[:, :HEAD_DIM] = (k * lax.rsqrt(ms + RMS_EPS) * gk_ref[...]).astype(BF16)
    nch = k.shape[0]
    end = lax.broadcasted_iota(jnp.int32, (nch, HEAD_DIM), 0) * CMP_STRIDE + (CMP_BLOCK - 1)
    col = lax.broadcasted_iota(jnp.int32, (nch, HEAD_DIM), 1)
    parts = jnp.where(col == 0, end // LANES * LANES, jnp.where(col == 1, end % LANES, 0))
    ko_ref[:, HEAD_DIM:] = parts.astype(F32).astype(BF16)
    vo_ref[...] = comp(vc_ref, wv_ref, pv_ref).T.astype(BF16)


def _compress(cv_ch, wk, wv, pk, pv, gk):
    _, b, nch, cw = cv_ch.shape
    g = NSA_KV_HEADS
    chunk_spec = lambda head: pl.BlockSpec((None, None, nch, cw), lambda bi, gi: (head + gi, bi, 0, 0))
    w_spec = pl.BlockSpec(wk.shape, lambda bi, gi: (0, 0))
    p_spec = pl.BlockSpec(pk.shape, lambda bi, gi: (0, 0))
    return pl.pallas_call(
        _compress_kernel,
        grid=(b, g),
        in_specs=[chunk_spec(CV_K), chunk_spec(CV_V), w_spec, w_spec, p_spec, p_spec,
                  pl.BlockSpec((1, HEAD_DIM), lambda bi, gi: (0, 0))],
        out_specs=[pl.BlockSpec((None, None, nch, 2 * HEAD_DIM), lambda bi, gi: (bi, gi, 0, 0)),
                   pl.BlockSpec((None, None, HEAD_DIM, nch), lambda bi, gi: (bi, gi, 0, 0))],
        out_shape=[jax.ShapeDtypeStruct((b, g, nch, 2 * HEAD_DIM), BF16),
                   jax.ShapeDtypeStruct((b, g, HEAD_DIM, nch), BF16)],
        compiler_params=_params("arbitrary", "arbitrary"),
        name="compress",
    )(cv_ch, cv_ch, wk, wv, pk, pv, gk)


def _online_update(s, v_t, m_ref, l_ref, acc_ref, exp):
    m_old = m_ref[...]
    m_new = jnp.maximum(m_old, jnp.max(s, axis=0, keepdims=True))
    alpha = exp(m_old - m_new)
    p = exp(s - m_new)
    l_ref[...] = alpha * l_ref[...] + jnp.sum(p, axis=0, keepdims=True)
    acc_ref[...] = alpha * acc_ref[...] + _dot(v_t, p.astype(BF16))
    m_ref[...] = m_new


def _reset(m_ref, l_ref, acc_ref):
    m_ref[...] = jnp.full(m_ref.shape, NEG_INF, F32)
    l_ref[...] = jnp.zeros(l_ref.shape, F32)
    acc_ref[...] = jnp.zeros(acc_ref.shape, F32)


FOX_TQ = 1024
FOX_TK = 256


def _fox_kernel(q_ref, k_ref, kb_ref, vt_ref, o_ref, kx_ref, qt_ref, s_ref, m_ref, l_ref, acc_ref, *, tq, tk):
    i = pl.program_id(2)

    @pl.when(i == 0)
    def _():
        kx_ref[:, :HEAD_DIM] = k_ref[...]
        kx_ref[:, HEAD_DIM:] = kb_ref[...]

    qt_ref[:HEAD_DIM, :] = q_ref[...]
    row = lax.broadcasted_iota(jnp.int32, (HEAD_DIM, tq), 0)
    qt_ref[HEAD_DIM:, :] = jnp.where(row < CUM_PIECES, 1.0, 0.0).astype(BF16)
    _reset(m_ref, l_ref, acc_ref)

    def scores(j, slot, c0=0):
        k0 = pl.multiple_of(j * tk, tk)
        s_ref[slot, :, c0:] = _dot(kx_ref[pl.ds(k0, tk), :], qt_ref[:, c0:])

    def consume(j, slot, diag, c0=0):
        k0 = pl.multiple_of(j * tk, tk)
        s = s_ref[slot, :, c0:]
        if diag:
            key = k0 + lax.broadcasted_iota(jnp.int32, s.shape, 0)
            t = i * tq + c0 + lax.broadcasted_iota(jnp.int32, s.shape, 1)
            s = jnp.where(key <= t, s, NEG_INF)
        _online_update(s, vt_ref[:, pl.ds(k0, tk)], m_ref.at[:, c0:], l_ref.at[:, c0:], acc_ref.at[:, c0:], jnp.exp2)

    n = tq // tk
    scores(0, 0)

    def group(gi, c):
        for d in range(n):
            scores(gi * n + d + 1, (d + 1) % 2)
            consume(gi * n + d, d % 2, False)
        return c

    lax.fori_loop(0, i, group, 0)
    for d in range(n):
        if d + 1 < n:
            scores(i * n + d + 1, (d + 1) % 2, (d + 1) * tk)
        consume(i * n + d, d % 2, True, d * tk)
    o = acc_ref[...] * (1.0 / l_ref[...])
    for c in range(tq // LANES):
        o_ref[c * LANES:(c + 1) * LANES, :] = o[:, c * LANES:(c + 1) * LANES].T.astype(o_ref.dtype)


def _fox(proj, cum, vt, tq, tk):
    b, t, _ = proj.shape
    assert (tq // tk) % 2 == 0 and t % tq == 0
    return pl.pallas_call(
        functools.partial(_fox_kernel, tq=tq, tk=tk),
        grid=(b, N_FOX_HEADS, t // tq),
        in_specs=[
            pl.BlockSpec((HEAD_DIM, tq), lambda bi, h, i: (VT_QF + h, bi * (t // tq) + i)),
            pl.BlockSpec((None, t, HEAD_DIM), lambda bi, h, i: (bi, 0, COL_KF + h)),
            pl.BlockSpec((None, None, t, LANES), lambda bi, h, i: (bi, h, 0, 0)),
            pl.BlockSpec((HEAD_DIM, t), lambda bi, h, i: (VT_VF + h, bi)),
        ],
        out_specs=pl.BlockSpec((None, tq, HEAD_DIM), lambda bi, h, i: (bi, i, h)),
        out_shape=jax.ShapeDtypeStruct((b, t, FOX_W), BF16),
        scratch_shapes=[
            pltpu.VMEM((t, 2 * HEAD_DIM), BF16),
            pltpu.VMEM((2 * HEAD_DIM, tq), BF16),
            pltpu.VMEM((2, tk, tq), F32),
            pltpu.VMEM((1, tq), F32),
            pltpu.VMEM((1, tq), F32),
            pltpu.VMEM((HEAD_DIM, tq), F32),
        ],
        compiler_params=_params("arbitrary", "arbitrary", "arbitrary"),
        name="fox",
    )(vt, proj, cum, vt)


NSA_TQ = 128
SLC_TK = 256
MASK_BIAS = -1e9


def _nsa_kernel(slopes_ref, q_ref, kc_ref, vct_ref, ks_ref, kw_ref, pos_ref, vst_ref, vwt_ref, gate_ref, o_ref,
                ksx_ref, kwx_ref, qc_ref, qw_ref, qs_ref, s_ref, m_ref, l_ref, acc_ref, tile_ref, *, tq, n_slc):
    g = pl.program_id(1)
    i = pl.program_id(2)
    q0 = i * tq
    R = NSA_GROUP
    rt = R * tq
    ncp = kc_ref.shape[0]
    nsp = LANES
    slopes = [slopes_ref[g * R + r] for r in range(R)]

    @pl.when(i == 0)
    def _():
        ksx_ref[:, :HEAD_DIM] = ks_ref[...]
        ksx_ref[:, HEAD_DIM:] = pos_ref[...]
        kwx_ref[:, :HEAD_DIM] = kw_ref[...]
        kwx_ref[:, HEAD_DIM:] = pos_ref[...]

    col = lax.broadcasted_iota(jnp.int32, (1, rt), 1)
    rid = col // tq
    slope_row = jnp.where(rid == 0, slopes[0], jnp.where(rid == 1, slopes[1], jnp.where(rid == 2, slopes[2], slopes[3])))
    t_row = q0 + col - rid * tq

    for r in range(R):
        qr = q_ref[r * HEAD_DIM:(r + 1) * HEAD_DIM, :]
        for dst in (qc_ref, qw_ref, qs_ref):
            dst[:HEAD_DIM, r * tq:(r + 1) * tq] = qr
    row2 = lax.broadcasted_iota(jnp.int32, (HEAD_DIM, rt), 0)
    qc_ref[HEAD_DIM:, :] = jnp.where(row2 < 2, slope_row, 0.0).astype(BF16)
    pos_aug = jnp.where(row2 == 0, slope_row, slope_row * (SLC_BLOCK * row2).astype(F32))
    qw_ref[HEAD_DIM:, :] = pos_aug.astype(BF16)

    last_c = (t_row - (CMP_BLOCK - 1)) >> CMP_SHIFT
    mask_c = lax.broadcasted_iota(jnp.int32, (ncp, rt), 0) <= last_c
    s = jnp.where(mask_c, _dot(kc_ref[...], qc_ref[...]), NEG_INF)
    mx = jnp.max(s, axis=0, keepdims=True)
    p = jnp.exp(s - mx)
    inv_l = jnp.where(last_c >= 0, 1.0 / jnp.maximum(jnp.sum(p, axis=0, keepdims=True), 1e-30), 0.0)
    p = p * inv_l
    o_cmp = _dot(vct_ref[...], p.astype(BF16))
    p_sum = p[:, 0:tq]
    for r in range(1, R):
        p_sum = p_sum + p[:, r * tq:(r + 1) * tq]

    sj = lax.broadcasted_iota(jnp.int32, (nsp, ncp), 0)
    cn = lax.broadcasted_iota(jnp.int32, (nsp, ncp), 1)
    ov = (cn * CMP_STRIDE < (sj + 1) * SLC_BLOCK) & (cn * CMP_STRIDE + CMP_BLOCK > sj * SLC_BLOCK)
    ov = jnp.where(ov & (cn < ncp - 1) & (sj < n_slc), 1.0, 0.0).astype(BF16)
    imp = sum(_dot(ov, piece) for piece in _split_bf16(p_sum, CUM_PIECES))
    blk = lax.broadcasted_iota(jnp.int32, (nsp, tq), 0)
    tpos = q0 + lax.broadcasted_iota(jnp.int32, (nsp, tq), 1)
    cur = tpos // SLC_BLOCK
    forced = (blk == 0) | (blk == cur) | (blk == cur - 1)
    valid = blk * SLC_BLOCK <= tpos
    work = jnp.where(forced, FORCE_SCORE, jnp.where(valid, imp, -FORCE_SCORE))
    for _ in range(min(SLC_TOPK, n_slc)):
        mx = jnp.max(work, axis=0, keepdims=True)
        first = jnp.min(jnp.where(work == mx, blk, nsp), axis=0, keepdims=True)
        work = jnp.where(blk == first, -jnp.inf, work)
    sel = work == -jnp.inf

    for r in range(R):
        cols = slice(r * tq, (r + 1) * tq)
        aug = jnp.where(sel | (blk == 0), pos_aug[:, cols], MASK_BIAS)
        qs_ref[HEAD_DIM:, cols] = aug.astype(BF16)

    span = WINDOW + tq
    w0 = pl.multiple_of(jnp.maximum(q0 - WINDOW, 0), tq)
    rel = lax.broadcasted_iota(jnp.int32, (span, rt), 0)
    mask_w = (rel <= t_row - w0) & (rel > t_row - w0 - WINDOW)
    s = jnp.where(mask_w, _dot(kwx_ref[pl.ds(w0, span), :], qw_ref[...]), NEG_INF)
    mx = jnp.max(s, axis=0, keepdims=True)
    p = jnp.exp(s - mx)
    o_win = _dot(vwt_ref[:, pl.ds(w0, span)], p.astype(BF16))
    inv_l_win = 1.0 / jnp.sum(p, axis=0, keepdims=True)

    gt = gate_ref[...].T
    o_part = []
    for r in range(R):
        cols = slice(r * tq, (r + 1) * tq)
        o_part.append(gt[3 * r:3 * r + 1] * o_cmp[:, cols]
                      + (gt[3 * r + 2:3 * r + 3] * inv_l_win[:, cols]) * o_win[:, cols])

    _reset(m_ref, l_ref, acc_ref)

    def scores(j, slot):
        k0 = pl.multiple_of(j * SLC_TK, SLC_TK)
        s_ref[slot] = _dot(ksx_ref[pl.ds(k0, SLC_TK), :], qs_ref[...])

    def consume(j, slot, diag):
        k0 = pl.multiple_of(j * SLC_TK, SLC_TK)
        s = s_ref[slot]
        if diag:
            key = k0 + lax.broadcasted_iota(jnp.int32, (SLC_TK, rt), 0)
            s = jnp.where(key <= t_row, s, NEG_INF)
        _online_update(s, vst_ref[:, pl.ds(k0, SLC_TK)], m_ref, l_ref, acc_ref, jnp.exp)

    jd = q0 // SLC_TK
    scores(jd, 0)

    bpt = SLC_TK // SLC_BLOCK
    hit = jnp.max(jnp.where(sel, 1.0, 0.0).T, axis=0, keepdims=True)
    shift = 1
    while shift < bpt:
        hit = jnp.maximum(hit, pltpu.roll(hit, shift=nsp - shift, axis=1))
        shift *= 2
    cnt = jnp.int32(0)
    for j in range(n_slc // bpt):
        tile_ref[cnt] = j
        cnt = cnt + ((hit[0, j * bpt] > 0.5) & (j < jd)).astype(jnp.int32)
    tile_ref[cnt] = jd

    scores(tile_ref[0], 1)
    consume(jd, 0, True)

    def pair(jj, c):
        scores(tile_ref[2 * jj + 1], 0)
        consume(tile_ref[2 * jj], 1, False)
        scores(tile_ref[2 * jj + 2], 1)
        consume(tile_ref[2 * jj + 1], 0, False)
        return c

    lax.fori_loop(0, cnt // 2, pair, 0)

    @pl.when(cnt % 2 == 1)
    def _():
        consume(tile_ref[cnt - 1], 1, False)

    inv_l_slc = 1.0 / l_ref[...]
    for r in range(R):
        cols = slice(r * tq, (r + 1) * tq)
        o = o_part[r] + (gt[3 * r + 1:3 * r + 2] * inv_l_slc[:, cols]) * acc_ref[:, cols]
        o_ref[:, r * HEAD_DIM:(r + 1) * HEAD_DIM] = o.T.astype(o_ref.dtype)


def _nsa(slopes, proj, k_cmp, v_cmp_t, pos_cols, vt, gates, tq):
    b, t, _ = proj.shape
    n_slc = t // SLC_BLOCK
    assert n_slc <= LANES and t % SLC_TK == 0 and t >= WINDOW + tq and SLC_TK % tq == 0 and tq == LANES
    ncp = k_cmp.shape[2]
    gw = NSA_GROUP * HEAD_DIM
    rt = NSA_GROUP * tq
    per_group = lambda shape: pl.BlockSpec((None, None) + shape, lambda bi, g, i, s: (bi, g, 0, 0))
    row_head = lambda col: pl.BlockSpec((None, t, HEAD_DIM), lambda bi, g, i, s: (bi, 0, col + g))
    vt_head = lambda hd: pl.BlockSpec((HEAD_DIM, t), lambda bi, g, i, s: (hd + g, bi))
    grid_spec = pltpu.PrefetchScalarGridSpec(
        num_scalar_prefetch=1,
        grid=(b, NSA_KV_HEADS, t // tq),
        in_specs=[
            pl.BlockSpec((gw, tq), lambda bi, g, i, s: (VT_QN // NSA_GROUP + g, bi * (t // tq) + i)),
            per_group((ncp, 2 * HEAD_DIM)), per_group((HEAD_DIM, ncp)),
            row_head(COL_KS), row_head(COL_KW),
            pl.BlockSpec((t, LANES), lambda bi, g, i, s: (0, 0)),
            vt_head(VT_VS), vt_head(VT_VW),
            pl.BlockSpec((None, tq, LANES), lambda bi, g, i, s: (bi, i, g)),
        ],
        out_specs=pl.BlockSpec((None, tq, gw), lambda bi, g, i, s: (bi, i, g)),
        scratch_shapes=[
            pltpu.VMEM((t, 2 * HEAD_DIM), BF16),
            pltpu.VMEM((t, 2 * HEAD_DIM), BF16),
            pltpu.VMEM((2 * HEAD_DIM, rt), BF16),
            pltpu.VMEM((2 * HEAD_DIM, rt), BF16),
            pltpu.VMEM((2 * HEAD_DIM, rt), BF16),
            pltpu.VMEM((2, SLC_TK, rt), F32),
            pltpu.VMEM((1, rt), F32),
            pltpu.VMEM((1, rt), F32),
            pltpu.VMEM((HEAD_DIM, rt), F32),
            pltpu.SMEM((LANES,), jnp.int32),
        ],
    )
    return pl.pallas_call(
        functools.partial(_nsa_kernel, tq=tq, n_slc=n_slc),
        grid_spec=grid_spec,
        out_shape=jax.ShapeDtypeStruct((b, t, NSA_W), BF16),
        compiler_params=_params("arbitrary", "arbitrary", "arbitrary"),
        name="nsa",
    )(slopes, vt, k_cmp, v_cmp_t, proj, proj, pos_cols, vt, vt, gates)


def _outproj_kernel(x_ref, of_ref, on_ref, wf_ref, wn_ref, o_ref):
    o_ref[...] = x_ref[...] + _dot(of_ref[...], wf_ref[...]) + _dot(on_ref[...], wn_ref[...])


def _outproj(x2, o_fox, o_nsa, w_fox, w_nsa, tm, tn):
    m, d = x2.shape
    return pl.pallas_call(
        _outproj_kernel,
        grid=(m // tm, d // tn),
        in_specs=[
            pl.BlockSpec((tm, tn), lambda i, j: (i, j)),
            pl.BlockSpec((tm, FOX_W), lambda i, j: (i, 0)),
            pl.BlockSpec((tm, NSA_W), lambda i, j: (i, 0)),
            pl.BlockSpec((FOX_W, tn), lambda i, j: (0, j)),
            pl.BlockSpec((NSA_W, tn), lambda i, j: (0, j)),
        ],
        out_specs=pl.BlockSpec((tm, tn), lambda i, j: (i, j)),
        out_shape=jax.ShapeDtypeStruct((m, d), F32),
        compiler_params=_params("arbitrary", "arbitrary"),
        name="outproj",
    )(x2, o_fox, o_nsa, w_fox, w_nsa)


def _ffn_up_kernel(x_ref, g_ref, wg_ref, wu_ref, o_ref, h_ref):
    @pl.when(pl.program_id(1) == 0)
    def _():
        x = x_ref[...]
        ms = jnp.mean(x * x, axis=-1, keepdims=True)
        h_ref[...] = (x * lax.rsqrt(ms + RMS_EPS) * g_ref[...]).astype(BF16)

    h = h_ref[...]
    a = _dot(h, wg_ref[...])
    u = _dot(h, wu_ref[...])
    o_ref[...] = (a / (1.0 + jnp.exp(-a)) * u).astype(o_ref.dtype)


def _ffn_up(x2, g, w_gate, w_up, tm, tn):
    m, d = x2.shape
    n = w_gate.shape[1]
    return pl.pallas_call(
        _ffn_up_kernel,
        grid=(m // tm, n // tn),
        in_specs=[
            pl.BlockSpec((tm, d), lambda i, j: (i, 0)),
            pl.BlockSpec((1, d), lambda i, j: (0, 0)),
            pl.BlockSpec((d, tn), lambda i, j: (0, j)),
            pl.BlockSpec((d, tn), lambda i, j: (0, j)),
        ],
        out_specs=pl.BlockSpec((tm, tn), lambda i, j: (i, j)),
        out_shape=jax.ShapeDtypeStruct((m, n), BF16),
        scratch_shapes=[pltpu.VMEM((tm, d), BF16)],
        compiler_params=_params("arbitrary", "arbitrary"),
        name="ffn_up",
    )(x2, g, w_gate, w_up)


def _ffn_down_kernel(x_ref, a_ref, w_ref, o_ref):
    o_ref[...] = x_ref[...] + _dot(a_ref[...], w_ref[...])


def _ffn_down(x2, act, w_down, tm, tn):
    m, d = x2.shape
    f = act.shape[1]
    return pl.pallas_call(
        _ffn_down_kernel,
        grid=(m // tm, d // tn),
        in_specs=[
            pl.BlockSpec((tm, tn), lambda i, j: (i, j)),
            pl.BlockSpec((tm, f), lambda i, j: (i, 0)),
            pl.BlockSpec((f, tn), lambda i, j: (0, j)),
        ],
        out_specs=pl.BlockSpec((tm, tn), lambda i, j: (i, j)),
        out_shape=jax.ShapeDtypeStruct((m, d), F32),
        compiler_params=_params("arbitrary", "arbitrary"),
        name="ffn_down",
    )(x2, act, w_down)


def _layer_weights(w_in, fox_forget_bias, fox_q_norm, fox_k_norm, nsa_q_norm, slc_k_norm, win_k_norm):
    kv_w = NSA_KV_HEADS * HEAD_DIM
    splits = (FOX_W, FOX_W, FOX_W, N_FOX_HEADS, NSA_W, kv_w, kv_w, kv_w, kv_w, kv_w, kv_w, 3 * N_NSA_HEADS)
    cuts = [0]
    for s in splits:
        cuts.append(cuts[-1] + s)
    part = lambda k: w_in[:, cuts[k]:cuts[k + 1]]
    w_all = jnp.concatenate([part(k) for k in (1, 7, 9, 5, 6, 0, 4, 2, 8, 10)], axis=1).astype(BF16)
    d = w_in.shape[0]
    w_aux = jnp.concatenate(
        [part(3), part(11), jnp.zeros((d, LANES - N_FOX_HEADS - 3 * N_NSA_HEADS), w_in.dtype)], axis=1).astype(BF16)

    scale = HEAD_DIM ** -0.5
    ones = lambda n: jnp.ones((n * HEAD_DIM,), F32)
    rep = lambda gvec, n, s=1.0: jnp.tile(gvec.astype(F32) * s, n)
    n_v = N_FOX_HEADS + 2 * NSA_KV_HEADS
    gain = jnp.concatenate([
        rep(fox_k_norm, N_FOX_HEADS), rep(slc_k_norm, NSA_KV_HEADS), rep(win_k_norm, NSA_KV_HEADS),
        ones(CV_HEADS),
        rep(fox_q_norm, N_FOX_HEADS, scale * LOG2E), rep(nsa_q_norm, N_NSA_HEADS, scale), ones(n_v)])[None, :]
    z = lambda n: jnp.zeros((n * HEAD_DIM,), F32)
    flag = jnp.concatenate([
        ones(ROW_HEADS), z(CV_HEADS), ones(N_FOX_HEADS + N_NSA_HEADS), z(n_v)])[None, :]
    bias = jnp.concatenate([fox_forget_bias.astype(F32), jnp.zeros((LANES - N_FOX_HEADS,), F32)])[None, :]
    return w_all, w_aux, gain, flag, bias


def _pick(n, prefs):
    for p in prefs:
        if n % p == 0:
            return p
    return n


def kernel(x, attn_norm, w_in, fox_forget_bias, fox_q_norm, fox_k_norm, nsa_q_norm, cmp_k_norm, slc_k_norm,
           win_k_norm, cmp_pos_k, cmp_pos_v, cmp_w_k, cmp_w_v, w_out, ffn_norm, w_gate, w_up, w_down):
    b, t, d = x.shape
    depth = w_in.shape[0]
    m = b * t
    tm = _pick(m, (ROW_TILE, 512, 256, 128))
    nch = t // CMP_STRIDE
    slopes = jnp.asarray([2.0 ** (-8.0 * (i + 1) / N_NSA_HEADS) for i in range(N_NSA_HEADS)], F32)

    key = jnp.arange(t)[:, None]
    col = jnp.arange(LANES)[None, :]
    pos_cols = jnp.where(col == 0, key % SLC_BLOCK, key // SLC_BLOCK == col).astype(BF16)

    x2 = x.reshape(m, d)
    for l in range(depth):
        w_all, w_aux, gain, flag, bias = _layer_weights(
            w_in[l], fox_forget_bias[l], fox_q_norm[l], fox_k_norm[l], nsa_q_norm[l], slc_k_norm[l], win_k_norm[l])
        proj, cv, vt, aux = _inproj(x2, attn_norm[l][None, :], w_all, w_aux, gain, flag, tm, PROJ_COLS)
        proj = proj.reshape(b, t, ROW_HEADS * HEAD_DIM)
        cum, gates = _gates(aux.reshape(b, t, LANES), bias, _pick(t, (GATE_ROWS, 256, 128)))

        pos = lambda p: jnp.broadcast_to(p.reshape(1, CMP_BLOCK * HEAD_DIM), (8, CMP_BLOCK * HEAD_DIM))
        k_cmp, v_cmp_t = _compress(
            cv.reshape(CV_HEADS, b, nch, CMP_STRIDE * HEAD_DIM),
            cmp_w_k[l].reshape(CMP_BLOCK * HEAD_DIM, HEAD_DIM), cmp_w_v[l].reshape(CMP_BLOCK * HEAD_DIM, HEAD_DIM),
            pos(cmp_pos_k[l]), pos(cmp_pos_v[l]), cmp_k_norm[l][None, :].astype(F32))

        o_fox = _fox(proj, cum, vt, _pick(t, (FOX_TQ, 512)), FOX_TK)
        o_nsa = _nsa(slopes, proj, k_cmp, v_cmp_t, pos_cols, vt, gates, NSA_TQ)

        wo = w_out[l].astype(BF16)
        x2 = _outproj(x2, o_fox.reshape(m, FOX_W), o_nsa.reshape(m, NSA_W), wo[:FOX_W], wo[FOX_W:], tm, OUT_COLS)

        act = _ffn_up(x2, ffn_norm[l][None, :], w_gate[l].astype(BF16), w_up[l].astype(BF16), tm, FFN_COLS)
        x2 = _ffn_down(x2, act, w_down[l].astype(BF16), _pick(m, (DOWN_ROW_TILE, 256, 128)), OUT_COLS)
    return x2.reshape(b, t, d)
```

```python
import functools
import math

import jax
import jax.numpy as jnp
from jax import lax
from jax.experimental import pallas as pl
from jax.experimental.pallas import tpu as pltpu

HEAD_DIM = 128
N_FOX_HEADS = 8
N_NSA_HEADS = 8
NSA_GROUP = 4
NSA_KV_HEADS = N_NSA_HEADS // NSA_GROUP
CMP_BLOCK = 32
CMP_STRIDE = 16
CMP_SHIFT = CMP_STRIDE.bit_length() - 1
SLC_BLOCK = 64
SLC_TOPK = 16
WINDOW = 512
RMS_EPS = 1e-6
NEG_INF = -1e30
FORCE_SCORE = 1e9

FOX_W = N_FOX_HEADS * HEAD_DIM
NSA_W = N_NSA_HEADS * HEAD_DIM
LANES = 128
VMEM_LIMIT = 56 * 1024 * 1024
CUM_PIECES = 3
LOG2E = math.log2(math.e)
GATE_CHUNK = 256
ROW_TILE = 1024
PROJ_COLS = 512
OUT_COLS = 1024
FFN_COLS = 512
DOWN_ROW_TILE = 512
GATE_ROWS = 512

F32 = jnp.float32
BF16 = jnp.bfloat16

COL_KF = 0
COL_KS = COL_KF + N_FOX_HEADS
COL_KW = COL_KS + NSA_KV_HEADS
ROW_HEADS = COL_KW + NSA_KV_HEADS
CV_K = 0
CV_V = CV_K + NSA_KV_HEADS
CV_HEADS = CV_V + NSA_KV_HEADS
VT_QF = 0
VT_QN = VT_QF + N_FOX_HEADS
VT_VF = VT_QN + N_NSA_HEADS
VT_VS = VT_VF + N_FOX_HEADS
VT_VW = VT_VS + NSA_KV_HEADS
VT_HEADS = VT_VW + NSA_KV_HEADS


def _params(*sem):
    return pltpu.CompilerParams(dimension_semantics=sem, vmem_limit_bytes=VMEM_LIMIT)


def _dot(a, b):
    return jnp.dot(a, b, preferred_element_type=F32)


def _split_bf16(x, n):
    pieces, rest = [], x
    for _ in range(n):
        piece = rest.astype(BF16)
        pieces.append(piece)
        rest = rest - piece.astype(F32)
    return pieces


def _inproj_kernel(x_ref, g_ref, w_ref, waux_ref, gain_ref, flag_ref, o_ref, cv_ref, vt_ref, aux_ref, h_ref, acc_ref,
                   *, n_row, n_tiles):
    j = pl.program_id(1)

    @pl.when(j == 0)
    def _():
        x = x_ref[...]
        ms = jnp.mean(x * x, axis=-1, keepdims=True)
        h = (x * lax.rsqrt(ms + RMS_EPS) * g_ref[...]).astype(BF16)
        h_ref[...] = h
        aux_ref[...] = _dot(h, waux_ref[...])

    def matmul(slot):
        acc_ref[slot] = _dot(h_ref[...], w_ref[...])

    ROW, HEAD_MAJOR, TRANSPOSED = 0, 1, 2

    def finish(slot, kind):
        for c in range(acc_ref.shape[2] // HEAD_DIM):
            sl = slice(c * HEAD_DIM, (c + 1) * HEAD_DIM)
            y = acc_ref[slot, :, sl]
            ms = jnp.mean(y * y, axis=-1, keepdims=True)
            f = flag_ref[:, sl]
            inv = f * lax.rsqrt(ms + RMS_EPS) + (1.0 - f)
            y = y * inv * gain_ref[:, sl]
            if kind == TRANSPOSED:
                vt_ref[sl, :] = y.T.astype(vt_ref.dtype)
            elif kind == HEAD_MAJOR:
                cv_ref[c] = y.astype(cv_ref.dtype)
            else:
                o_ref[:, sl] = y.astype(o_ref.dtype)

    @pl.when(j == 0)
    def _():
        matmul(0)

    prev_kind = jnp.where(j - 1 < n_row, ROW, jnp.where(j - 1 == n_row, HEAD_MAJOR, TRANSPOSED))
    for parity in (0, 1):
        for kind in (ROW, HEAD_MAJOR, TRANSPOSED):
            cond = (j >= 1) & (j < n_tiles) & (j % 2 == parity) & (prev_kind == kind)

            @pl.when(cond)
            def _(parity=parity, kind=kind):
                matmul(parity)
                finish(1 - parity, kind)

    @pl.when(j == n_tiles)
    def _():
        finish((n_tiles - 1) % 2, TRANSPOSED)


def _inproj(x2, g, w_all, w_aux, gain, flag, tm, tn):
    m, d = x2.shape
    n_row = ROW_HEADS * HEAD_DIM // tn
    assert tn == CV_HEADS * HEAD_DIM
    n_tiles = n_row + 1 + VT_HEADS * HEAD_DIM // tn
    prev = lambda j: jnp.clip(j - 1, 0, n_tiles - 1)
    return pl.pallas_call(
        functools.partial(_inproj_kernel, n_row=n_row, n_tiles=n_tiles),
        grid=(m // tm, n_tiles + 1),
        in_specs=[
            pl.BlockSpec((tm, d), lambda i, j: (i, 0)),
            pl.BlockSpec((1, d), lambda i, j: (0, 0)),
            pl.BlockSpec((d, tn), lambda i, j: (0, jnp.minimum(j, n_tiles - 1))),
            pl.BlockSpec((d, LANES), lambda i, j: (0, 0)),
            pl.BlockSpec((1, tn), lambda i, j: (0, prev(j))),
            pl.BlockSpec((1, tn), lambda i, j: (0, prev(j))),
        ],
        out_specs=[
            pl.BlockSpec((tm, tn), lambda i, j: (i, jnp.minimum(prev(j), n_row - 1))),
            pl.BlockSpec((CV_HEADS, tm, HEAD_DIM), lambda i, j: (0, i, 0)),
            pl.BlockSpec((tn, tm), lambda i, j: (jnp.maximum(prev(j) - n_row - 1, 0), i)),
            pl.BlockSpec((tm, LANES), lambda i, j: (i, 0)),
        ],
        out_shape=[
            jax.ShapeDtypeStruct((m, ROW_HEADS * HEAD_DIM), BF16),
            jax.ShapeDtypeStruct((CV_HEADS, m, HEAD_DIM), BF16),
            jax.ShapeDtypeStruct((VT_HEADS * HEAD_DIM, m), BF16),
            jax.ShapeDtypeStruct((m, LANES), F32),
        ],
        scratch_shapes=[pltpu.VMEM((tm, d), BF16), pltpu.VMEM((2, tm, tn), F32)],
        compiler_params=_params("arbitrary", "arbitrary"),
        name="inproj",
    )(x2, g, w_all, w_aux, gain, flag)


def _gates_kernel(a_ref, b_ref, cum_ref, gate_ref, carry_ref):
    @pl.when(pl.program_id(1) == 0)
    def _():
        carry_ref[...] = jnp.zeros_like(carry_ref)

    a = a_ref[...]
    tc = a.shape[0]
    z = a + b_ref[...]
    lf = jnp.minimum(z, 0.0) - jnp.log(1.0 + jnp.exp(-jnp.abs(z)))
    rows = lax.broadcasted_iota(jnp.int32, (GATE_CHUNK, GATE_CHUNK), 0)
    cols = lax.broadcasted_iota(jnp.int32, (GATE_CHUNK, GATE_CHUNK), 1)
    tri = jnp.where(cols <= rows, 1.0, 0.0).astype(BF16)
    carry = carry_ref[...]
    chunks = []
    for c in range(tc // GATE_CHUNK):
        part = lf[c * GATE_CHUNK:(c + 1) * GATE_CHUNK]
        cs_c = sum(_dot(tri, piece) for piece in _split_bf16(part, CUM_PIECES)) + carry
        carry = cs_c[GATE_CHUNK - 1:GATE_CHUNK, :]
        chunks.append(cs_c)
    carry_ref[...] = carry
    cs = jnp.concatenate(chunks, axis=0)
    pieces = _split_bf16(cs * (-LOG2E), CUM_PIECES)
    lane = lax.broadcasted_iota(jnp.int32, (tc, LANES), 1)
    packed = jnp.zeros((tc, LANES), BF16)
    for k, piece in enumerate(pieces):
        moved = piece if k == 0 else pltpu.roll(piece.astype(F32), shift=k * N_FOX_HEADS, axis=1).astype(BF16)
        packed = jnp.where((lane >= k * N_FOX_HEADS) & (lane < (k + 1) * N_FOX_HEADS), moved, packed)
    src = lax.broadcasted_iota(jnp.int32, (LANES, N_FOX_HEADS * LANES), 0)
    dst = lax.broadcasted_iota(jnp.int32, (LANES, N_FOX_HEADS * LANES), 1)
    dst_h, dst_k = dst // LANES, dst % LANES
    route = jnp.where((dst_k < CUM_PIECES) & (src == dst_k * N_FOX_HEADS + dst_h), 1.0, 0.0).astype(BF16)
    routed = _dot(packed, route)
    for h in range(N_FOX_HEADS):
        cum_ref[h] = routed[:, h * LANES:(h + 1) * LANES].astype(cum_ref.dtype)
    sg = 1.0 / (1.0 + jnp.exp(-a))
    for g in range(NSA_KV_HEADS):
        first = N_FOX_HEADS + g * NSA_GROUP * 3
        gate_ref[:, g * LANES:(g + 1) * LANES] = pltpu.roll(sg, shift=LANES - first, axis=1)


def _gates(aux, bias, tc):
    b, t, _ = aux.shape
    return pl.pallas_call(
        _gates_kernel,
        grid=(b, t // tc),
        in_specs=[
            pl.BlockSpec((None, tc, LANES), lambda bi, i: (bi, i, 0)),
            pl.BlockSpec((1, LANES), lambda bi, i: (0, 0)),
        ],
        out_specs=[
            pl.BlockSpec((None, N_FOX_HEADS, tc, LANES), lambda bi, i: (bi, 0, i, 0)),
            pl.BlockSpec((None, tc, NSA_KV_HEADS * LANES), lambda bi, i: (bi, i, 0)),
        ],
        out_shape=[
            jax.ShapeDtypeStruct((b, N_FOX_HEADS, t, LANES), BF16),
            jax.ShapeDtypeStruct((b, t, NSA_KV_HEADS * LANES), F32),
        ],
        scratch_shapes=[pltpu.VMEM((1, LANES), F32)],
        compiler_params=_params("arbitrary", "arbitrary"),
        name="gates",
    )(aux, bias)


def _compress_kernel(kc_ref, vc_ref, wk_ref, wv_ref, pk_ref, pv_ref, gk_ref, ko_ref, vo_ref):
    def comp(c_ref, w_ref, p_ref):
        a = c_ref[...]
        w = w_ref[...]
        half = w.shape[0] // 2
        wb = w.astype(BF16)
        y0 = _dot(a, wb[:half])
        y1 = _dot(a, wb[half:])
        y1 = pltpu.roll(y1, shift=y1.shape[0] - 1, axis=0)
        pw = jnp.dot(p_ref[...], w, preferred_element_type=F32, precision=lax.Precision.HIGHEST)
        return y0 + y1 + pw[0:1, :]

    k = comp(kc_ref, wk_ref, pk_ref)
    ms = jnp.mean(k * k, axis=-1, keepdims=True)
    ko_ref[:, :HEAD_DIM] = (k * lax.rsqrt(ms + RMS_EPS) * gk_ref[...]).astype(BF16)
    nch = k.shape[0]
    end = lax.broadcasted_iota(jnp.int32, (nch, HEAD_DIM), 0) * CMP_STRIDE + (CMP_BLOCK - 1)
    col = lax.broadcasted_iota(jnp.int32, (nch, HEAD_DIM), 1)
    parts = jnp.where(col == 0, end // LANES * LANES, jnp.where(col == 1, end % LANES, 0))
    ko_ref[:, HEAD_DIM:] = parts.astype(F32).astype(BF16)
    vo_ref[...] = comp(vc_ref, wv_ref, pv_ref).T.astype(BF16)


def _compress(cv_ch, wk, wv, pk, pv, gk):
    _, b, nch, cw = cv_ch.shape
    g = NSA_KV_HEADS
    chunk_spec = lambda head: pl.BlockSpec((None, None, nch, cw), lambda bi, gi: (head + gi, bi, 0, 0))
    w_spec = pl.BlockSpec(wk.shape, lambda bi, gi: (0, 0))
    p_spec = pl.BlockSpec(pk.shape, lambda bi, gi: (0, 0))
    return pl.pallas_call(
        _compress_kernel,
        grid=(b, g),
        in_specs=[chunk_spec(CV_K), chunk_spec(CV_V), w_spec, w_spec, p_spec, p_spec,
                  pl.BlockSpec((1, HEAD_DIM), lambda bi, gi: (0, 0))],
        out_specs=[pl.BlockSpec((None, None, nch, 2 * HEAD_DIM), lambda bi, gi: (bi, gi, 0, 0)),
                   pl.BlockSpec((None, None, HEAD_DIM, nch), lambda bi, gi: (bi, gi, 0, 0))],
        out_shape=[jax.ShapeDtypeStruct((b, g, nch, 2 * HEAD_DIM), BF16),
                   jax.ShapeDtypeStruct((b, g, HEAD_DIM, nch), BF16)],
        compiler_params=_params("arbitrary", "arbitrary"),
        name="compress",
    )(cv_ch, cv_ch, wk, wv, pk, pv, gk)


def _online_update(s, v_t, m_ref, l_ref, acc_ref, exp):
    m_old = m_ref[...]
    m_new = jnp.maximum(m_old, jnp.max(s, axis=0, keepdims=True))
    alpha = exp(m_old - m_new)
    p = exp(s - m_new)
    l_ref[...] = alpha * l_ref[...] + jnp.sum(p, axis=0, keepdims=True)
    acc_ref[...] = alpha * acc_ref[...] + _dot(v_t, p.astype(BF16))
    m_ref[...] = m_new


def _reset(m_ref, l_ref, acc_ref):
    m_ref[...] = jnp.full(m_ref.shape, NEG_INF, F32)
    l_ref[...] = jnp.zeros(l_ref.shape, F32)
    acc_ref[...] = jnp.zeros(acc_ref.shape, F32)


FOX_TQ = 1024
FOX_TK = 256


def _fox_kernel(q_ref, k_ref, kb_ref, vt_ref, o_ref, kx_ref, qt_ref, s_ref, m_ref, l_ref, acc_ref, *, tq, tk):
    i = pl.program_id(2)

    @pl.when(i == 0)
    def _():
        kx_ref[:, :HEAD_DIM] = k_ref[...]
        kx_ref[:, HEAD_DIM:] = kb_ref[...]

    qt_ref[:HEAD_DIM, :] = q_ref[...]
    row = lax.broadcasted_iota(jnp.int32, (HEAD_DIM, tq), 0)
    qt_ref[HEAD_DIM:, :] = jnp.where(row < CUM_PIECES, 1.0, 0.0).astype(BF16)
    _reset(m_ref, l_ref, acc_ref)

    def scores(j, slot, c0=0):
        k0 = pl.multiple_of(j * tk, tk)
        s_ref[slot, :, c0:] = _dot(kx_ref[pl.ds(k0, tk), :], qt_ref[:, c0:])

    def consume(j, slot, diag, c0=0):
        k0 = pl.multiple_of(j * tk, tk)
        s = s_ref[slot, :, c0:]
        if diag:
            key = k0 + lax.broadcasted_iota(jnp.int32, s.shape, 0)
            t = i * tq + c0 + lax.broadcasted_iota(jnp.int32, s.shape, 1)
            s = jnp.where(key <= t, s, NEG_INF)
        _online_update(s, vt_ref[:, pl.ds(k0, tk)], m_ref.at[:, c0:], l_ref.at[:, c0:], acc_ref.at[:, c0:], jnp.exp2)

    n = tq // tk
    scores(0, 0)

    def group(gi, c):
        for d in range(n):
            scores(gi * n + d + 1, (d + 1) % 2)
            consume(gi * n + d, d % 2, False)
        return c

    lax.fori_loop(0, i, group, 0)
    for d in range(n):
        if d + 1 < n:
            scores(i * n + d + 1, (d + 1) % 2, (d + 1) * tk)
        consume(i * n + d, d % 2, True, d * tk)
    o = acc_ref[...] * (1.0 / l_ref[...])
    for c in range(tq // LANES):
        o_ref[c * LANES:(c + 1) * LANES, :] = o[:, c * LANES:(c + 1) * LANES].T.astype(o_ref.dtype)


def _fox(proj, cum, vt, tq, tk):
    b, t, _ = proj.shape
    assert (tq // tk) % 2 == 0 and t % tq == 0
    return pl.pallas_call(
        functools.partial(_fox_kernel, tq=tq, tk=tk),
        grid=(b, N_FOX_HEADS, t // tq),
        in_specs=[
            pl.BlockSpec((HEAD_DIM, tq), lambda bi, h, i: (VT_QF + h, bi * (t // tq) + i)),
            pl.BlockSpec((None, t, HEAD_DIM), lambda bi, h, i: (bi, 0, COL_KF + h)),
            pl.BlockSpec((None, None, t, LANES), lambda bi, h, i: (bi, h, 0, 0)),
            pl.BlockSpec((HEAD_DIM, t), lambda bi, h, i: (VT_VF + h, bi)),
        ],
        out_specs=pl.BlockSpec((None, tq, HEAD_DIM), lambda bi, h, i: (bi, i, h)),
        out_shape=jax.ShapeDtypeStruct((b, t, FOX_W), BF16),
        scratch_shapes=[
            pltpu.VMEM((t, 2 * HEAD_DIM), BF16),
            pltpu.VMEM((2 * HEAD_DIM, tq), BF16),
            pltpu.VMEM((2, tk, tq), F32),
            pltpu.VMEM((1, tq), F32),
            pltpu.VMEM((1, tq), F32),
            pltpu.VMEM((HEAD_DIM, tq), F32),
        ],
        compiler_params=_params("arbitrary", "arbitrary", "arbitrary"),
        name="fox",
    )(vt, proj, cum, vt)


NSA_TQ = 128
SLC_TK = 256
MASK_BIAS = -1e9


def _nsa_kernel(slopes_ref, q_ref, kc_ref, vct_ref, ks_ref, kw_ref, pos_ref, vst_ref, vwt_ref, gate_ref, o_ref,
                ksx_ref, kwx_ref, qc_ref, qw_ref, qs_ref, s_ref, m_ref, l_ref, acc_ref, tile_ref, *, tq, n_slc):
    g = pl.program_id(1)
    i = pl.program_id(2)
    q0 = i * tq
    R = NSA_GROUP
    rt = R * tq
    ncp = kc_ref.shape[0]
    nsp = LANES
    slopes = [slopes_ref[g * R + r] for r in range(R)]

    @pl.when(i == 0)
    def _():
        ksx_ref[:, :HEAD_DIM] = ks_ref[...]
        ksx_ref[:, HEAD_DIM:] = pos_ref[...]
        kwx_ref[:, :HEAD_DIM] = kw_ref[...]
        kwx_ref[:, HEAD_DIM:] = pos_ref[...]

    col = lax.broadcasted_iota(jnp.int32, (1, rt), 1)
    rid = col // tq
    slope_row = jnp.where(rid == 0, slopes[0], jnp.where(rid == 1, slopes[1], jnp.where(rid == 2, slopes[2], slopes[3])))
    t_row = q0 + col - rid * tq

    for r in range(R):
        qr = q_ref[r * HEAD_DIM:(r + 1) * HEAD_DIM, :]
        for dst in (qc_ref, qw_ref, qs_ref):
            dst[:HEAD_DIM, r * tq:(r + 1) * tq] = qr
    row2 = lax.broadcasted_iota(jnp.int32, (HEAD_DIM, rt), 0)
    qc_ref[HEAD_DIM:, :] = jnp.where(row2 < 2, slope_row, 0.0).astype(BF16)
    pos_aug = jnp.where(row2 == 0, slope_row, slope_row * (SLC_BLOCK * row2).astype(F32))
    qw_ref[HEAD_DIM:, :] = pos_aug.astype(BF16)

    last_c = (t_row - (CMP_BLOCK - 1)) >> CMP_SHIFT
    mask_c = lax.broadcasted_iota(jnp.int32, (ncp, rt), 0) <= last_c
    s = jnp.where(mask_c, _dot(kc_ref[...], qc_ref[...]), NEG_INF)
    mx = jnp.max(s, axis=0, keepdims=True)
    p = jnp.exp(s - mx)
    inv_l = jnp.where(last_c >= 0, 1.0 / jnp.maximum(jnp.sum(p, axis=0, keepdims=True), 1e-30), 0.0)
    p = p * inv_l
    o_cmp = _dot(vct_ref[...], p.astype(BF16))
    p_sum = p[:, 0:tq]
    for r in range(1, R):
        p_sum = p_sum + p[:, r * tq:(r + 1) * tq]

    sj = lax.broadcasted_iota(jnp.int32, (nsp, ncp), 0)
    cn = lax.broadcasted_iota(jnp.int32, (nsp, ncp), 1)
    ov = (cn * CMP_STRIDE < (sj + 1) * SLC_BLOCK) & (cn * CMP_STRIDE + CMP_BLOCK > sj * SLC_BLOCK)
    ov = jnp.where(ov & (cn < ncp - 1) & (sj < n_slc), 1.0, 0.0).astype(BF16)
    imp = sum(_dot(ov, piece) for piece in _split_bf16(p_sum, CUM_PIECES))
    blk = lax.broadcasted_iota(jnp.int32, (nsp, tq), 0)
    tpos = q0 + lax.broadcasted_iota(jnp.int32, (nsp, tq), 1)
    cur = tpos // SLC_BLOCK
    forced = (blk == 0) | (blk == cur) | (blk == cur - 1)
    valid = blk * SLC_BLOCK <= tpos
    work = jnp.where(forced, FORCE_SCORE, jnp.where(valid, imp, -FORCE_SCORE))
    for _ in range(min(SLC_TOPK, n_slc)):
        mx = jnp.max(work, axis=0, keepdims=True)
        first = jnp.min(jnp.where(work == mx, blk, nsp), axis=0, keepdims=True)
        work = jnp.where(blk == first, -jnp.inf, work)
    sel = work == -jnp.inf

    for r in range(R):
        cols = slice(r * tq, (r + 1) * tq)
        aug = jnp.where(sel | (blk == 0), pos_aug[:, cols], MASK_BIAS)
        qs_ref[HEAD_DIM:, cols] = aug.astype(BF16)

    span = WINDOW + tq
    w0 = pl.multiple_of(jnp.maximum(q0 - WINDOW, 0), tq)
    rel = lax.broadcasted_iota(jnp.int32, (span, rt), 0)
    mask_w = (rel <= t_row - w0) & (rel > t_row - w0 - WINDOW)
    s = jnp.where(mask_w, _dot(kwx_ref[pl.ds(w0, span), :], qw_ref[...]), NEG_INF)
    mx = jnp.max(s, axis=0, keepdims=True)
    p = jnp.exp(s - mx)
    o_win = _dot(vwt_ref[:, pl.ds(w0, span)], p.astype(BF16))
    inv_l_win = 1.0 / jnp.sum(p, axis=0, keepdims=True)

    gt = gate_ref[...].T
    o_part = []
    for r in range(R):
        cols = slice(r * tq, (r + 1) * tq)
        o_part.append(gt[3 * r:3 * r + 1] * o_cmp[:, cols]
                      + (gt[3 * r + 2:3 * r + 3] * inv_l_win[:, cols]) * o_win[:, cols])

    _reset(m_ref, l_ref, acc_ref)

    def scores(j, slot):
        k0 = pl.multiple_of(j * SLC_TK, SLC_TK)
        s_ref[slot] = _dot(ksx_ref[pl.ds(k0, SLC_TK), :], qs_ref[...])

    def consume(j, slot, diag):
        k0 = pl.multiple_of(j * SLC_TK, SLC_TK)
        s = s_ref[slot]
        if diag:
            key = k0 + lax.broadcasted_iota(jnp.int32, (SLC_TK, rt), 0)
            s = jnp.where(key <= t_row, s, NEG_INF)
        _online_update(s, vst_ref[:, pl.ds(k0, SLC_TK)], m_ref, l_ref, acc_ref, jnp.exp)

    jd = q0 // SLC_TK
    scores(jd, 0)

    bpt = SLC_TK // SLC_BLOCK
    hit = jnp.max(jnp.where(sel, 1.0, 0.0).T, axis=0, keepdims=True)
    shift = 1
    while shift < bpt:
        hit = jnp.maximum(hit, pltpu.roll(hit, shift=nsp - shift, axis=1))
        shift *= 2
    cnt = jnp.int32(0)
    for j in range(n_slc // bpt):
        tile_ref[cnt] = j
        cnt = cnt + ((hit[0, j * bpt] > 0.5) & (j < jd)).astype(jnp.int32)
    tile_ref[cnt] = jd

    scores(tile_ref[0], 1)
    consume(jd, 0, True)

    def run(first, n):
        for d in range(n):
            scores(tile_ref[first + d + 1], d % 2)
            consume(tile_ref[first + d], (d + 1) % 2, False)

    lax.fori_loop(0, cnt // 4, lambda qq, c: (run(4 * qq, 4), c)[1], 0)
    rest = cnt - cnt % 4

    @pl.when(cnt % 4 >= 2)
    def _():
        run(rest, 2)

    @pl.when(cnt % 2 == 1)
    def _():
        consume(tile_ref[cnt - 1], 1, False)

    inv_l_slc = 1.0 / l_ref[...]
    for r in range(R):
        cols = slice(r * tq, (r + 1) * tq)
        o = o_part[r] + (gt[3 * r + 1:3 * r + 2] * inv_l_slc[:, cols]) * acc_ref[:, cols]
        o_ref[:, r * HEAD_DIM:(r + 1) * HEAD_DIM] = o.T.astype(o_ref.dtype)


def _nsa(slopes, proj, k_cmp, v_cmp_t, pos_cols, vt, gates, tq):
    b, t, _ = proj.shape
    n_slc = t // SLC_BLOCK
    assert n_slc <= LANES and t % SLC_TK == 0 and t >= WINDOW + tq and SLC_TK % tq == 0 and tq == LANES
    ncp = k_cmp.shape[2]
    gw = NSA_GROUP * HEAD_DIM
    rt = NSA_GROUP * tq
    per_group = lambda shape: pl.BlockSpec((None, None) + shape, lambda bi, g, i, s: (bi, g, 0, 0))
    row_head = lambda col: pl.BlockSpec((None, t, HEAD_DIM), lambda bi, g, i, s: (bi, 0, col + g))
    vt_head = lambda hd: pl.BlockSpec((HEAD_DIM, t), lambda bi, g, i, s: (hd + g, bi))
    grid_spec = pltpu.PrefetchScalarGridSpec(
        num_scalar_prefetch=1,
        grid=(b, NSA_KV_HEADS, t // tq),
        in_specs=[
            pl.BlockSpec((gw, tq), lambda bi, g, i, s: (VT_QN // NSA_GROUP + g, bi * (t // tq) + i)),
            per_group((ncp, 2 * HEAD_DIM)), per_group((HEAD_DIM, ncp)),
            row_head(COL_KS), row_head(COL_KW),
            pl.BlockSpec((t, LANES), lambda bi, g, i, s: (0, 0)),
            vt_head(VT_VS), vt_head(VT_VW),
            pl.BlockSpec((None, tq, LANES), lambda bi, g, i, s: (bi, i, g)),
        ],
        out_specs=pl.BlockSpec((None, tq, gw), lambda bi, g, i, s: (bi, i, g)),
        scratch_shapes=[
            pltpu.VMEM((t, 2 * HEAD_DIM), BF16),
            pltpu.VMEM((t, 2 * HEAD_DIM), BF16),
            pltpu.VMEM((2 * HEAD_DIM, rt), BF16),
            pltpu.VMEM((2 * HEAD_DIM, rt), BF16),
            pltpu.VMEM((2 * HEAD_DIM, rt), BF16),
            pltpu.VMEM((2, SLC_TK, rt), F32),
            pltpu.VMEM((1, rt), F32),
            pltpu.VMEM((1, rt), F32),
            pltpu.VMEM((HEAD_DIM, rt), F32),
            pltpu.SMEM((LANES,), jnp.int32),
        ],
    )
    return pl.pallas_call(
        functools.partial(_nsa_kernel, tq=tq, n_slc=n_slc),
        grid_spec=grid_spec,
        out_shape=jax.ShapeDtypeStruct((b, t, NSA_W), BF16),
        compiler_params=_params("arbitrary", "arbitrary", "arbitrary"),
        name="nsa",
    )(slopes, vt, k_cmp, v_cmp_t, proj, proj, pos_cols, vt, vt, gates)


def _outproj_kernel(x_ref, of_ref, on_ref, wf_ref, wn_ref, o_ref):
    o_ref[...] = x_ref[...] + _dot(of_ref[...], wf_ref[...]) + _dot(on_ref[...], wn_ref[...])


def _outproj(x2, o_fox, o_nsa, w_fox, w_nsa, tm, tn):
    m, d = x2.shape
    return pl.pallas_call(
        _outproj_kernel,
        grid=(m // tm, d // tn),
        in_specs=[
            pl.BlockSpec((tm, tn), lambda i, j: (i, j)),
            pl.BlockSpec((tm, FOX_W), lambda i, j: (i, 0)),
            pl.BlockSpec((tm, NSA_W), lambda i, j: (i, 0)),
            pl.BlockSpec((FOX_W, tn), lambda i, j: (0, j)),
            pl.BlockSpec((NSA_W, tn), lambda i, j: (0, j)),
        ],
        out_specs=pl.BlockSpec((tm, tn), lambda i, j: (i, j)),
        out_shape=jax.ShapeDtypeStruct((m, d), F32),
        compiler_params=_params("arbitrary", "arbitrary"),
        name="outproj",
    )(x2, o_fox, o_nsa, w_fox, w_nsa)


def _ffn_up_kernel(x_ref, g_ref, wg_ref, wu_ref, o_ref, h_ref):
    @pl.when(pl.program_id(1) == 0)
    def _():
        x = x_ref[...]
        ms = jnp.mean(x * x, axis=-1, keepdims=True)
        h_ref[...] = (x * lax.rsqrt(ms + RMS_EPS) * g_ref[...]).astype(BF16)

    h = h_ref[...]
    a = _dot(h, wg_ref[...])
    u = _dot(h, wu_ref[...])
    o_ref[...] = (a / (1.0 + jnp.exp(-a)) * u).astype(o_ref.dtype)


def _ffn_up(x2, g, w_gate, w_up, tm, tn):
    m, d = x2.shape
    n = w_gate.shape[1]
    return pl.pallas_call(
        _ffn_up_kernel,
        grid=(m // tm, n // tn),
        in_specs=[
            pl.BlockSpec((tm, d), lambda i, j: (i, 0)),
            pl.BlockSpec((1, d), lambda i, j: (0, 0)),
            pl.BlockSpec((d, tn), lambda i, j: (0, j)),
            pl.BlockSpec((d, tn), lambda i, j: (0, j)),
        ],
        out_specs=pl.BlockSpec((tm, tn), lambda i, j: (i, j)),
        out_shape=jax.ShapeDtypeStruct((m, n), BF16),
        scratch_shapes=[pltpu.VMEM((tm, d), BF16)],
        compiler_params=_params("arbitrary", "arbitrary"),
        name="ffn_up",
    )(x2, g, w_gate, w_up)


def _ffn_down_kernel(x_ref, a_ref, w_ref, o_ref):
    o_ref[...] = x_ref[...] + _dot(a_ref[...], w_ref[...])


def _ffn_down(x2, act, w_down, tm, tn):
    m, d = x2.shape
    f = act.shape[1]
    return pl.pallas_call(
        _ffn_down_kernel,
        grid=(m // tm, d // tn),
        in_specs=[
            pl.BlockSpec((tm, tn), lambda i, j: (i, j)),
            pl.BlockSpec((tm, f), lambda i, j: (i, 0)),
            pl.BlockSpec((f, tn), lambda i, j: (0, j)),
        ],
        out_specs=pl.BlockSpec((tm, tn), lambda i, j: (i, j)),
        out_shape=jax.ShapeDtypeStruct((m, d), F32),
        compiler_params=_params("arbitrary", "arbitrary"),
        name="ffn_down",
    )(x2, act, w_down)


def _layer_weights(w_in, fox_forget_bias, fox_q_norm, fox_k_norm, nsa_q_norm, slc_k_norm, win_k_norm):
    kv_w = NSA_KV_HEADS * HEAD_DIM
    splits = (FOX_W, FOX_W, FOX_W, N_FOX_HEADS, NSA_W, kv_w, kv_w, kv_w, kv_w, kv_w, kv_w, 3 * N_NSA_HEADS)
    cuts = [0]
    for s in splits:
        cuts.append(cuts[-1] + s)
    part = lambda k: w_in[:, cuts[k]:cuts[k + 1]]
    w_all = jnp.concatenate([part(k) for k in (1, 7, 9, 5, 6, 0, 4, 2, 8, 10)], axis=1).astype(BF16)
    d = w_in.shape[0]
    w_aux = jnp.concatenate(
        [part(3), part(11), jnp.zeros((d, LANES - N_FOX_HEADS - 3 * N_NSA_HEADS), w_in.dtype)], axis=1).astype(BF16)

    scale = HEAD_DIM ** -0.5
    ones = lambda n: jnp.ones((n * HEAD_DIM,), F32)
    rep = lambda gvec, n, s=1.0: jnp.tile(gvec.astype(F32) * s, n)
    n_v = N_FOX_HEADS + 2 * NSA_KV_HEADS
    gain = jnp.concatenate([
        rep(fox_k_norm, N_FOX_HEADS), rep(slc_k_norm, NSA_KV_HEADS), rep(win_k_norm, NSA_KV_HEADS),
        ones(CV_HEADS),
        rep(fox_q_norm, N_FOX_HEADS, scale * LOG2E), rep(nsa_q_norm, N_NSA_HEADS, scale), ones(n_v)])[None, :]
    z = lambda n: jnp.zeros((n * HEAD_DIM,), F32)
    flag = jnp.concatenate([
        ones(ROW_HEADS), z(CV_HEADS), ones(N_FOX_HEADS + N_NSA_HEADS), z(n_v)])[None, :]
    bias = jnp.concatenate([fox_forget_bias.astype(F32), jnp.zeros((LANES - N_FOX_HEADS,), F32)])[None, :]
    return w_all, w_aux, gain, flag, bias


def _pick(n, prefs):
    for p in prefs:
        if n % p == 0:
            return p
    return n


def kernel(x, attn_norm, w_in, fox_forget_bias, fox_q_norm, fox_k_norm, nsa_q_norm, cmp_k_norm, slc_k_norm,
           win_k_norm, cmp_pos_k, cmp_pos_v, cmp_w_k, cmp_w_v, w_out, ffn_norm, w_gate, w_up, w_down):
    b, t, d = x.shape
    depth = w_in.shape[0]
    m = b * t
    tm = _pick(m, (ROW_TILE, 512, 256, 128))
    nch = t // CMP_STRIDE
    slopes = jnp.asarray([2.0 ** (-8.0 * (i + 1) / N_NSA_HEADS) for i in range(N_NSA_HEADS)], F32)

    key = jnp.arange(t)[:, None]
    col = jnp.arange(LANES)[None, :]
    pos_cols = jnp.where(col == 0, key % SLC_BLOCK, key // SLC_BLOCK == col).astype(BF16)

    x2 = x.reshape(m, d)
    for l in range(depth):
        w_all, w_aux, gain, flag, bias = _layer_weights(
            w_in[l], fox_forget_bias[l], fox_q_norm[l], fox_k_norm[l], nsa_q_norm[l], slc_k_norm[l], win_k_norm[l])
        proj, cv, vt, aux = _inproj(x2, attn_norm[l][None, :], w_all, w_aux, gain, flag, tm, PROJ_COLS)
        proj = proj.reshape(b, t, ROW_HEADS * HEAD_DIM)
        cum, gates = _gates(aux.reshape(b, t, LANES), bias, _pick(t, (GATE_ROWS, 256, 128)))

        pos = lambda p: jnp.broadcast_to(p.reshape(1, CMP_BLOCK * HEAD_DIM), (8, CMP_BLOCK * HEAD_DIM))
        k_cmp, v_cmp_t = _compress(
            cv.reshape(CV_HEADS, b, nch, CMP_STRIDE * HEAD_DIM),
            cmp_w_k[l].reshape(CMP_BLOCK * HEAD_DIM, HEAD_DIM), cmp_w_v[l].reshape(CMP_BLOCK * HEAD_DIM, HEAD_DIM),
            pos(cmp_pos_k[l]), pos(cmp_pos_v[l]), cmp_k_norm[l][None, :].astype(F32))

        o_fox = _fox(proj, cum, vt, _pick(t, (FOX_TQ, 512)), FOX_TK)
        o_nsa = _nsa(slopes, proj, k_cmp, v_cmp_t, pos_cols, vt, gates, NSA_TQ)

        wo = w_out[l].astype(BF16)
        x2 = _outproj(x2, o_fox.reshape(m, FOX_W), o_nsa.reshape(m, NSA_W), wo[:FOX_W], wo[FOX_W:], tm, OUT_COLS)

        act = _ffn_up(x2, ffn_norm[l][None, :], w_gate[l].astype(BF16), w_up[l].astype(BF16), tm, FFN_COLS)
        x2 = _ffn_down(x2, act, w_down[l].astype(BF16), _pick(m, (DOWN_ROW_TILE, 256, 128)), OUT_COLS)
    return x2.reshape(b, t, d)
```

```python
import functools
import math

import jax
import jax.numpy as jnp
from jax import lax
from jax.experimental import pallas as pl
from jax.experimental.pallas import tpu as pltpu

HEAD_DIM = 128
N_FOX_HEADS = 8
N_NSA_HEADS = 8
NSA_GROUP = 4
NSA_KV_HEADS = N_NSA_HEADS // NSA_GROUP
CMP_BLOCK = 32
CMP_STRIDE = 16
CMP_SHIFT = CMP_STRIDE.bit_length() - 1
SLC_BLOCK = 64
SLC_TOPK = 16
WINDOW = 512
RMS_EPS = 1e-6
NEG_INF = -1e30
FORCE_SCORE = 1e9

FOX_W = N_FOX_HEADS * HEAD_DIM
NSA_W = N_NSA_HEADS * HEAD_DIM
LANES = 128
VMEM_LIMIT = 56 * 1024 * 1024
CUM_PIECES = 3
LOG2E = math.log2(math.e)
GATE_CHUNK = 256
ROW_TILE = 1024
PROJ_COLS = 512
OUT_COLS = 1024
FFN_COLS = 512
DOWN_ROW_TILE = 512
GATE_ROWS = 512

F32 = jnp.float32
BF16 = jnp.bfloat16

COL_KF = 0
COL_KS = COL_KF + N_FOX_HEADS
COL_KW = COL_KS + NSA_KV_HEADS
ROW_HEADS = COL_KW + NSA_KV_HEADS
CV_K = 0
CV_V = CV_K + NSA_KV_HEADS
CV_HEADS = CV_V + NSA_KV_HEADS
VT_QF = 0
VT_QN = VT_QF + N_FOX_HEADS
VT_VF = VT_QN + N_NSA_HEADS
VT_VS = VT_VF + N_FOX_HEADS
VT_VW = VT_VS + NSA_KV_HEADS
VT_HEADS = VT_VW + NSA_KV_HEADS


def _params(*sem):
    return pltpu.CompilerParams(dimension_semantics=sem, vmem_limit_bytes=VMEM_LIMIT)


def _dot(a, b):
    return jnp.dot(a, b, preferred_element_type=F32)


def _split_bf16(x, n):
    pieces, rest = [], x
    for _ in range(n):
        piece = rest.astype(BF16)
        pieces.append(piece)
        rest = rest - piece.astype(F32)
    return pieces


def _inproj_kernel(x_ref, g_ref, w_ref, waux_ref, gain_ref, flag_ref, o_ref, cv_ref, vt_ref, aux_ref, h_ref, acc_ref,
                   *, n_row, n_tiles):
    j = pl.program_id(1)

    @pl.when(j == 0)
    def _():
        x = x_ref[...]
        ms = jnp.mean(x * x, axis=-1, keepdims=True)
        h = (x * lax.rsqrt(ms + RMS_EPS) * g_ref[...]).astype(BF16)
        h_ref[...] = h
        aux_ref[...] = _dot(h, waux_ref[...])

    def matmul(slot):
        acc_ref[slot] = _dot(h_ref[...], w_ref[...])

    ROW, HEAD_MAJOR, TRANSPOSED = 0, 1, 2

    def finish(slot, kind):
        for c in range(acc_ref.shape[2] // HEAD_DIM):
            sl = slice(c * HEAD_DIM, (c + 1) * HEAD_DIM)
            y = acc_ref[slot, :, sl]
            ms = jnp.mean(y * y, axis=-1, keepdims=True)
            f = flag_ref[:, sl]
            inv = f * lax.rsqrt(ms + RMS_EPS) + (1.0 - f)
            y = y * inv * gain_ref[:, sl]
            if kind == TRANSPOSED:
                vt_ref[sl, :] = y.T.astype(vt_ref.dtype)
            elif kind == HEAD_MAJOR:
                cv_ref[c] = y.astype(cv_ref.dtype)
            else:
                o_ref[:, sl] = y.astype(o_ref.dtype)

    @pl.when(j == 0)
    def _():
        matmul(0)

    prev_kind = jnp.where(j - 1 < n_row, ROW, jnp.where(j - 1 == n_row, HEAD_MAJOR, TRANSPOSED))
    for parity in (0, 1):
        for kind in (ROW, HEAD_MAJOR, TRANSPOSED):
            cond = (j >= 1) & (j < n_tiles) & (j % 2 == parity) & (prev_kind == kind)

            @pl.when(cond)
            def _(parity=parity, kind=kind):
                matmul(parity)
                finish(1 - parity, kind)

    @pl.when(j == n_tiles)
    def _():
        finish((n_tiles - 1) % 2, TRANSPOSED)


def _inproj(x2, g, w_all, w_aux, gain, flag, tm, tn):
    m, d = x2.shape
    n_row = ROW_HEADS * HEAD_DIM // tn
    assert tn == CV_HEADS * HEAD_DIM
    n_tiles = n_row + 1 + VT_HEADS * HEAD_DIM // tn
    prev = lambda j: jnp.clip(j - 1, 0, n_tiles - 1)
    return pl.pallas_call(
        functools.partial(_inproj_kernel, n_row=n_row, n_tiles=n_tiles),
        grid=(m // tm, n_tiles + 1),
        in_specs=[
            pl.BlockSpec((tm, d), lambda i, j: (i, 0)),
            pl.BlockSpec((1, d), lambda i, j: (0, 0)),
            pl.BlockSpec((d, tn), lambda i, j: (0, jnp.minimum(j, n_tiles - 1))),
            pl.BlockSpec((d, LANES), lambda i, j: (0, 0)),
            pl.BlockSpec((1, tn), lambda i, j: (0, prev(j))),
            pl.BlockSpec((1, tn), lambda i, j: (0, prev(j))),
        ],
        out_specs=[
            pl.BlockSpec((tm, tn), lambda i, j: (i, jnp.minimum(prev(j), n_row - 1))),
            pl.BlockSpec((CV_HEADS, tm, HEAD_DIM), lambda i, j: (0, i, 0)),
            pl.BlockSpec((tn, tm), lambda i, j: (jnp.maximum(prev(j) - n_row - 1, 0), i)),
            pl.BlockSpec((tm, LANES), lambda i, j: (i, 0)),
        ],
        out_shape=[
            jax.ShapeDtypeStruct((m, ROW_HEADS * HEAD_DIM), BF16),
            jax.ShapeDtypeStruct((CV_HEADS, m, HEAD_DIM), BF16),
            jax.ShapeDtypeStruct((VT_HEADS * HEAD_DIM, m), BF16),
            jax.ShapeDtypeStruct((m, LANES), F32),
        ],
        scratch_shapes=[pltpu.VMEM((tm, d), BF16), pltpu.VMEM((2, tm, tn), F32)],
        compiler_params=_params("arbitrary", "arbitrary"),
        name="inproj",
    )(x2, g, w_all, w_aux, gain, flag)


def _gates_kernel(a_ref, b_ref, cum_ref, gate_ref, carry_ref):
    @pl.when(pl.program_id(1) == 0)
    def _():
        carry_ref[...] = jnp.zeros_like(carry_ref)

    a = a_ref[...]
    tc = a.shape[0]
    z = a + b_ref[...]
    lf = jnp.minimum(z, 0.0) - jnp.log(1.0 + jnp.exp(-jnp.abs(z)))
    rows = lax.broadcasted_iota(jnp.int32, (GATE_CHUNK, GATE_CHUNK), 0)
    cols = lax.broadcasted_iota(jnp.int32, (GATE_CHUNK, GATE_CHUNK), 1)
    tri = jnp.where(cols <= rows, 1.0, 0.0).astype(BF16)
    carry = carry_ref[...]
    chunks = []
    for c in range(tc // GATE_CHUNK):
        part = lf[c * GATE_CHUNK:(c + 1) * GATE_CHUNK]
        cs_c = sum(_dot(tri, piece) for piece in _split_bf16(part, CUM_PIECES)) + carry
        carry = cs_c[GATE_CHUNK - 1:GATE_CHUNK, :]
        chunks.append(cs_c)
    carry_ref[...] = carry
    cs = jnp.concatenate(chunks, axis=0)
    pieces = _split_bf16(cs * (-LOG2E), CUM_PIECES)
    lane = lax.broadcasted_iota(jnp.int32, (tc, LANES), 1)
    packed = jnp.zeros((tc, LANES), BF16)
    for k, piece in enumerate(pieces):
        moved = piece if k == 0 else pltpu.roll(piece.astype(F32), shift=k * N_FOX_HEADS, axis=1).astype(BF16)
        packed = jnp.where((lane >= k * N_FOX_HEADS) & (lane < (k + 1) * N_FOX_HEADS), moved, packed)
    src = lax.broadcasted_iota(jnp.int32, (LANES, N_FOX_HEADS * LANES), 0)
    dst = lax.broadcasted_iota(jnp.int32, (LANES, N_FOX_HEADS * LANES), 1)
    dst_h, dst_k = dst // LANES, dst % LANES
    route = jnp.where((dst_k < CUM_PIECES) & (src == dst_k * N_FOX_HEADS + dst_h), 1.0, 0.0).astype(BF16)
    routed = _dot(packed, route)
    for h in range(N_FOX_HEADS):
        cum_ref[h] = routed[:, h * LANES:(h + 1) * LANES].astype(cum_ref.dtype)
    sg = 1.0 / (1.0 + jnp.exp(-a))
    for g in range(NSA_KV_HEADS):
        first = N_FOX_HEADS + g * NSA_GROUP * 3
        gate_ref[:, g * LANES:(g + 1) * LANES] = pltpu.roll(sg, shift=LANES - first, axis=1)


def _gates(aux, bias, tc):
    b, t, _ = aux.shape
    return pl.pallas_call(
        _gates_kernel,
        grid=(b, t // tc),
        in_specs=[
            pl.BlockSpec((None, tc, LANES), lambda bi, i: (bi, i, 0)),
            pl.BlockSpec((1, LANES), lambda bi, i: (0, 0)),
        ],
        out_specs=[
            pl.BlockSpec((None, N_FOX_HEADS, tc, LANES), lambda bi, i: (bi, 0, i, 0)),
            pl.BlockSpec((None, tc, NSA_KV_HEADS * LANES), lambda bi, i: (bi, i, 0)),
        ],
        out_shape=[
            jax.ShapeDtypeStruct((b, N_FOX_HEADS, t, LANES), BF16),
            jax.ShapeDtypeStruct((b, t, NSA_KV_HEADS * LANES), F32),
        ],
        scratch_shapes=[pltpu.VMEM((1, LANES), F32)],
        compiler_params=_params("arbitrary", "arbitrary"),
        name="gates",
    )(aux, bias)


def _compress_kernel(kc_ref, vc_ref, wk_ref, wv_ref, pk_ref, pv_ref, gk_ref, ko_ref, vo_ref):
    def comp(c_ref, w_ref, p_ref):
        a = c_ref[...]
        w = w_ref[...]
        half = w.shape[0] // 2
        wb = w.astype(BF16)
        y0 = _dot(a, wb[:half])
        y1 = _dot(a, wb[half:])
        y1 = pltpu.roll(y1, shift=y1.shape[0] - 1, axis=0)
        pw = jnp.dot(p_ref[...], w, preferred_element_type=F32, precision=lax.Precision.HIGHEST)
        return y0 + y1 + pw[0:1, :]

    k = comp(kc_ref, wk_ref, pk_ref)
    ms = jnp.mean(k * k, axis=-1, keepdims=True)
    ko_ref[:, :HEAD_DIM] = (k * lax.rsqrt(ms + RMS_EPS) * gk_ref[...]).astype(BF16)
    nch = k.shape[0]
    end = lax.broadcasted_iota(jnp.int32, (nch, HEAD_DIM), 0) * CMP_STRIDE + (CMP_BLOCK - 1)
    col = lax.broadcasted_iota(jnp.int32, (nch, HEAD_DIM), 1)
    parts = jnp.where(col == 0, end // LANES * LANES, jnp.where(col == 1, end % LANES, 0))
    ko_ref[:, HEAD_DIM:] = parts.astype(F32).astype(BF16)
    vo_ref[...] = comp(vc_ref, wv_ref, pv_ref).T.astype(BF16)


def _compress(cv_ch, wk, wv, pk, pv, gk):
    _, b, nch, cw = cv_ch.shape
    g = NSA_KV_HEADS
    chunk_spec = lambda head: pl.BlockSpec((None, None, nch, cw), lambda bi, gi: (head + gi, bi, 0, 0))
    w_spec = pl.BlockSpec(wk.shape, lambda bi, gi: (0, 0))
    p_spec = pl.BlockSpec(pk.shape, lambda bi, gi: (0, 0))
    return pl.pallas_call(
        _compress_kernel,
        grid=(b, g),
        in_specs=[chunk_spec(CV_K), chunk_spec(CV_V), w_spec, w_spec, p_spec, p_spec,
                  pl.BlockSpec((1, HEAD_DIM), lambda bi, gi: (0, 0))],
        out_specs=[pl.BlockSpec((None, None, nch, 2 * HEAD_DIM), lambda bi, gi: (bi, gi, 0, 0)),
                   pl.BlockSpec((None, None, HEAD_DIM, nch), lambda bi, gi: (bi, gi, 0, 0))],
        out_shape=[jax.ShapeDtypeStruct((b, g, nch, 2 * HEAD_DIM), BF16),
                   jax.ShapeDtypeStruct((b, g, HEAD_DIM, nch), BF16)],
        compiler_params=_params("arbitrary", "arbitrary"),
        name="compress",
    )(cv_ch, cv_ch, wk, wv, pk, pv, gk)


def _online_update(s, v_t, m_ref, l_ref, acc_ref, exp):
    m_old = m_ref[...]
    m_new = jnp.maximum(m_old, jnp.max(s, axis=0, keepdims=True))
    alpha = exp(m_old - m_new)
    p = exp(s - m_new)
    l_ref[...] = alpha * l_ref[...] + jnp.sum(p, axis=0, keepdims=True)
    acc_ref[...] = alpha * acc_ref[...] + _dot(v_t, p.astype(BF16))
    m_ref[...] = m_new


def _reset(m_ref, l_ref, acc_ref):
    m_ref[...] = jnp.full(m_ref.shape, NEG_INF, F32)
    l_ref[...] = jnp.zeros(l_ref.shape, F32)
    acc_ref[...] = jnp.zeros(acc_ref.shape, F32)


FOX_TQ = 1024
FOX_TK = 256


def _fox_kernel(q_ref, k_ref, kb_ref, vt_ref, o_ref, kx_ref, qt_ref, s_ref, m_ref, l_ref, acc_ref, *, tq, tk):
    i = pl.program_id(2)

    @pl.when(i == 0)
    def _():
        kx_ref[:, :HEAD_DIM] = k_ref[...]
        kx_ref[:, HEAD_DIM:] = kb_ref[...]

    qt_ref[:HEAD_DIM, :] = q_ref[...]
    row = lax.broadcasted_iota(jnp.int32, (HEAD_DIM, tq), 0)
    qt_ref[HEAD_DIM:, :] = jnp.where(row < CUM_PIECES, 1.0, 0.0).astype(BF16)
    _reset(m_ref, l_ref, acc_ref)

    def scores(j, slot, c0=0):
        k0 = pl.multiple_of(j * tk, tk)
        s_ref[slot, :, c0:] = _dot(kx_ref[pl.ds(k0, tk), :], qt_ref[:, c0:])

    def consume(j, slot, diag, c0=0):
        k0 = pl.multiple_of(j * tk, tk)
        s = s_ref[slot, :, c0:]
        if diag:
            key = k0 + lax.broadcasted_iota(jnp.int32, s.shape, 0)
            t = i * tq + c0 + lax.broadcasted_iota(jnp.int32, s.shape, 1)
            s = jnp.where(key <= t, s, NEG_INF)
        _online_update(s, vt_ref[:, pl.ds(k0, tk)], m_ref.at[:, c0:], l_ref.at[:, c0:], acc_ref.at[:, c0:], jnp.exp2)

    n = tq // tk
    scores(0, 0)

    def group(gi, c):
        for d in range(n):
            scores(gi * n + d + 1, (d + 1) % 2)
            consume(gi * n + d, d % 2, False)
        return c

    lax.fori_loop(0, i // 2, lambda gg, c: group(2 * gg + 1, group(2 * gg, c)), 0)

    @pl.when(i % 2 == 1)
    def _():
        group(i - 1, 0)

    for d in range(n):
        if d + 1 < n:
            scores(i * n + d + 1, (d + 1) % 2, (d + 1) * tk)
        consume(i * n + d, d % 2, True, d * tk)
    o = acc_ref[...] * (1.0 / l_ref[...])
    for c in range(tq // LANES):
        o_ref[c * LANES:(c + 1) * LANES, :] = o[:, c * LANES:(c + 1) * LANES].T.astype(o_ref.dtype)


def _fox(proj, cum, vt, tq, tk):
    b, t, _ = proj.shape
    assert (tq // tk) % 2 == 0 and t % tq == 0
    return pl.pallas_call(
        functools.partial(_fox_kernel, tq=tq, tk=tk),
        grid=(b, N_FOX_HEADS, t // tq),
        in_specs=[
            pl.BlockSpec((HEAD_DIM, tq), lambda bi, h, i: (VT_QF + h, bi * (t // tq) + i)),
            pl.BlockSpec((None, t, HEAD_DIM), lambda bi, h, i: (bi, 0, COL_KF + h)),
            pl.BlockSpec((None, None, t, LANES), lambda bi, h, i: (bi, h, 0, 0)),
            pl.BlockSpec((HEAD_DIM, t), lambda bi, h, i: (VT_VF + h, bi)),
        ],
        out_specs=pl.BlockSpec((None, tq, HEAD_DIM), lambda bi, h, i: (bi, i, h)),
        out_shape=jax.ShapeDtypeStruct((b, t, FOX_W), BF16),
        scratch_shapes=[
            pltpu.VMEM((t, 2 * HEAD_DIM), BF16),
            pltpu.VMEM((2 * HEAD_DIM, tq), BF16),
            pltpu.VMEM((2, tk, tq), F32),
            pltpu.VMEM((1, tq), F32),
            pltpu.VMEM((1, tq), F32),
            pltpu.VMEM((HEAD_DIM, tq), F32),
        ],
        compiler_params=_params("arbitrary", "arbitrary", "arbitrary"),
        name="fox",
    )(vt, proj, cum, vt)


NSA_TQ = 128
SLC_TK = 256
MASK_BIAS = -1e9


def _nsa_kernel(slopes_ref, q_ref, kc_ref, vct_ref, ks_ref, kw_ref, pos_ref, vst_ref, vwt_ref, gate_ref, o_ref,
                ksx_ref, kwx_ref, qc_ref, qw_ref, qs_ref, s_ref, m_ref, l_ref, acc_ref, tile_ref, *, tq, n_slc):
    g = pl.program_id(1)
    i = pl.program_id(2)
    q0 = i * tq
    R = NSA_GROUP
    rt = R * tq
    ncp = kc_ref.shape[0]
    nsp = LANES
    slopes = [slopes_ref[g * R + r] for r in range(R)]

    @pl.when(i == 0)
    def _():
        ksx_ref[:, :HEAD_DIM] = ks_ref[...]
        ksx_ref[:, HEAD_DIM:] = pos_ref[...]
        kwx_ref[:, :HEAD_DIM] = kw_ref[...]
        kwx_ref[:, HEAD_DIM:] = pos_ref[...]

    col = lax.broadcasted_iota(jnp.int32, (1, rt), 1)
    rid = col // tq
    slope_row = jnp.where(rid == 0, slopes[0], jnp.where(rid == 1, slopes[1], jnp.where(rid == 2, slopes[2], slopes[3])))
    t_row = q0 + col - rid * tq

    for r in range(R):
        qr = q_ref[r * HEAD_DIM:(r + 1) * HEAD_DIM, :]
        for dst in (qc_ref, qw_ref, qs_ref):
            dst[:HEAD_DIM, r * tq:(r + 1) * tq] = qr
    row2 = lax.broadcasted_iota(jnp.int32, (HEAD_DIM, rt), 0)
    qc_ref[HEAD_DIM:, :] = jnp.where(row2 < 2, slope_row, 0.0).astype(BF16)
    pos_aug = jnp.where(row2 == 0, slope_row, slope_row * (SLC_BLOCK * row2).astype(F32))
    qw_ref[HEAD_DIM:, :] = pos_aug.astype(BF16)

    last_c = (t_row - (CMP_BLOCK - 1)) >> CMP_SHIFT
    mask_c = lax.broadcasted_iota(jnp.int32, (ncp, rt), 0) <= last_c
    s = jnp.where(mask_c, _dot(kc_ref[...], qc_ref[...]), NEG_INF)
    mx = jnp.max(s, axis=0, keepdims=True)
    p = jnp.exp(s - mx)
    inv_l = jnp.where(last_c >= 0, 1.0 / jnp.maximum(jnp.sum(p, axis=0, keepdims=True), 1e-30), 0.0)
    p = p * inv_l
    o_cmp = _dot(vct_ref[...], p.astype(BF16))
    p_sum = p[:, 0:tq]
    for r in range(1, R):
        p_sum = p_sum + p[:, r * tq:(r + 1) * tq]

    sj = lax.broadcasted_iota(jnp.int32, (nsp, ncp), 0)
    cn = lax.broadcasted_iota(jnp.int32, (nsp, ncp), 1)
    ov = (cn * CMP_STRIDE < (sj + 1) * SLC_BLOCK) & (cn * CMP_STRIDE + CMP_BLOCK > sj * SLC_BLOCK)
    ov = jnp.where(ov & (cn < ncp - 1) & (sj < n_slc), 1.0, 0.0).astype(BF16)
    imp = sum(_dot(ov, piece) for piece in _split_bf16(p_sum, CUM_PIECES))
    blk = lax.broadcasted_iota(jnp.int32, (nsp, tq), 0)
    tpos = q0 + lax.broadcasted_iota(jnp.int32, (nsp, tq), 1)
    cur = tpos // SLC_BLOCK
    forced = (blk == 0) | (blk == cur) | (blk == cur - 1)
    valid = blk * SLC_BLOCK <= tpos
    work = jnp.where(forced, FORCE_SCORE, jnp.where(valid, imp, -FORCE_SCORE))
    for _ in range(min(SLC_TOPK, n_slc)):
        mx = jnp.max(work, axis=0, keepdims=True)
        first = jnp.min(jnp.where(work == mx, blk, nsp), axis=0, keepdims=True)
        work = jnp.where(blk == first, -jnp.inf, work)
    sel = work == -jnp.inf

    for r in range(R):
        cols = slice(r * tq, (r + 1) * tq)
        aug = jnp.where(sel | (blk == 0), pos_aug[:, cols], MASK_BIAS)
        qs_ref[HEAD_DIM:, cols] = aug.astype(BF16)

    span = WINDOW + tq
    w0 = pl.multiple_of(jnp.maximum(q0 - WINDOW, 0), tq)
    rel = lax.broadcasted_iota(jnp.int32, (span, rt), 0)
    mask_w = (rel <= t_row - w0) & (rel > t_row - w0 - WINDOW)
    s = jnp.where(mask_w, _dot(kwx_ref[pl.ds(w0, span), :], qw_ref[...]), NEG_INF)
    mx = jnp.max(s, axis=0, keepdims=True)
    p = jnp.exp(s - mx)
    o_win = _dot(vwt_ref[:, pl.ds(w0, span)], p.astype(BF16))
    inv_l_win = 1.0 / jnp.sum(p, axis=0, keepdims=True)

    gt = gate_ref[...].T
    o_part = []
    for r in range(R):
        cols = slice(r * tq, (r + 1) * tq)
        o_part.append(gt[3 * r:3 * r + 1] * o_cmp[:, cols]
                      + (gt[3 * r + 2:3 * r + 3] * inv_l_win[:, cols]) * o_win[:, cols])

    _reset(m_ref, l_ref, acc_ref)

    def scores(j, slot):
        k0 = pl.multiple_of(j * SLC_TK, SLC_TK)
        s_ref[slot] = _dot(ksx_ref[pl.ds(k0, SLC_TK), :], qs_ref[...])

    def consume(j, slot, diag):
        k0 = pl.multiple_of(j * SLC_TK, SLC_TK)
        s = s_ref[slot]
        if diag:
            key = k0 + lax.broadcasted_iota(jnp.int32, (SLC_TK, rt), 0)
            s = jnp.where(key <= t_row, s, NEG_INF)
        _online_update(s, vst_ref[:, pl.ds(k0, SLC_TK)], m_ref, l_ref, acc_ref, jnp.exp)

    jd = q0 // SLC_TK
    scores(jd, 0)

    bpt = SLC_TK // SLC_BLOCK
    hit = jnp.max(jnp.where(sel, 1.0, 0.0).T, axis=0, keepdims=True)
    shift = 1
    while shift < bpt:
        hit = jnp.maximum(hit, pltpu.roll(hit, shift=nsp - shift, axis=1))
        shift *= 2
    cnt = jnp.int32(0)
    for j in range(n_slc // bpt):
        tile_ref[cnt] = j
        cnt = cnt + ((hit[0, j * bpt] > 0.5) & (j < jd)).astype(jnp.int32)
    tile_ref[cnt] = jd

    scores(tile_ref[0], 1)
    consume(jd, 0, True)

    def run(first, n):
        for d in range(n):
            scores(tile_ref[first + d + 1], d % 2)
            consume(tile_ref[first + d], (d + 1) % 2, False)

    lax.fori_loop(0, cnt // 4, lambda qq, c: (run(4 * qq, 4), c)[1], 0)
    rest = cnt - cnt % 4

    @pl.when(cnt % 4 >= 2)
    def _():
        run(rest, 2)

    @pl.when(cnt % 2 == 1)
    def _():
        consume(tile_ref[cnt - 1], 1, False)

    inv_l_slc = 1.0 / l_ref[...]
    for r in range(R):
        cols = slice(r * tq, (r + 1) * tq)
        o = o_part[r] + (gt[3 * r + 1:3 * r + 2] * inv_l_slc[:, cols]) * acc_ref[:, cols]
        o_ref[:, r * HEAD_DIM:(r + 1) * HEAD_DIM] = o.T.astype(o_ref.dtype)


def _nsa(slopes, proj, k_cmp, v_cmp_t, pos_cols, vt, gates, tq):
    b, t, _ = proj.shape
    n_slc = t // SLC_BLOCK
    assert n_slc <= LANES and t % SLC_TK == 0 and t >= WINDOW + tq and SLC_TK % tq == 0 and tq == LANES
    ncp = k_cmp.shape[2]
    gw = NSA_GROUP * HEAD_DIM
    rt = NSA_GROUP * tq
    per_group = lambda shape: pl.BlockSpec((None, None) + shape, lambda bi, g, i, s: (bi, g, 0, 0))
    row_head = lambda col: pl.BlockSpec((None, t, HEAD_DIM), lambda bi, g, i, s: (bi, 0, col + g))
    vt_head = lambda hd: pl.BlockSpec((HEAD_DIM, t), lambda bi, g, i, s: (hd + g, bi))
    grid_spec = pltpu.PrefetchScalarGridSpec(
        num_scalar_prefetch=1,
        grid=(b, NSA_KV_HEADS, t // tq),
        in_specs=[
            pl.BlockSpec((gw, tq), lambda bi, g, i, s: (VT_QN // NSA_GROUP + g, bi * (t // tq) + i)),
            per_group((ncp, 2 * HEAD_DIM)), per_group((HEAD_DIM, ncp)),
            row_head(COL_KS), row_head(COL_KW),
            pl.BlockSpec((t, LANES), lambda bi, g, i, s: (0, 0)),
            vt_head(VT_VS), vt_head(VT_VW),
            pl.BlockSpec((None, tq, LANES), lambda bi, g, i, s: (bi, i, g)),
        ],
        out_specs=pl.BlockSpec((None, tq, gw), lambda bi, g, i, s: (bi, i, g)),
        scratch_shapes=[
            pltpu.VMEM((t, 2 * HEAD_DIM), BF16),
            pltpu.VMEM((t, 2 * HEAD_DIM), BF16),
            pltpu.VMEM((2 * HEAD_DIM, rt), BF16),
            pltpu.VMEM((2 * HEAD_DIM, rt), BF16),
            pltpu.VMEM((2 * HEAD_DIM, rt), BF16),
            pltpu.VMEM((2, SLC_TK, rt), F32),
            pltpu.VMEM((1, rt), F32),
            pltpu.VMEM((1, rt), F32),
            pltpu.VMEM((HEAD_DIM, rt), F32),
            pltpu.SMEM((LANES,), jnp.int32),
        ],
    )
    return pl.pallas_call(
        functools.partial(_nsa_kernel, tq=tq, n_slc=n_slc),
        grid_spec=grid_spec,
        out_shape=jax.ShapeDtypeStruct((b, t, NSA_W), BF16),
        compiler_params=_params("arbitrary", "arbitrary", "arbitrary"),
        name="nsa",
    )(slopes, vt, k_cmp, v_cmp_t, proj, proj, pos_cols, vt, vt, gates)


def _outproj_kernel(x_ref, of_ref, on_ref, wf_ref, wn_ref, o_ref):
    o_ref[...] = x_ref[...] + _dot(of_ref[...], wf_ref[...]) + _dot(on_ref[...], wn_ref[...])


def _outproj(x2, o_fox, o_nsa, w_fox, w_nsa, tm, tn):
    m, d = x2.shape
    return pl.pallas_call(
        _outproj_kernel,
        grid=(m // tm, d // tn),
        in_specs=[
            pl.BlockSpec((tm, tn), lambda i, j: (i, j)),
            pl.BlockSpec((tm, FOX_W), lambda i, j: (i, 0)),
            pl.BlockSpec((tm, NSA_W), lambda i, j: (i, 0)),
            pl.BlockSpec((FOX_W, tn), lambda i, j: (0, j)),
            pl.BlockSpec((NSA_W, tn), lambda i, j: (0, j)),
        ],
        out_specs=pl.BlockSpec((tm, tn), lambda i, j: (i, j)),
        out_shape=jax.ShapeDtypeStruct((m, d), F32),
        compiler_params=_params("arbitrary", "arbitrary"),
        name="outproj",
    )(x2, o_fox, o_nsa, w_fox, w_nsa)


def _ffn_up_kernel(x_ref, g_ref, wg_ref, wu_ref, o_ref, h_ref):
    @pl.when(pl.program_id(1) == 0)
    def _():
        x = x_ref[...]
        ms = jnp.mean(x * x, axis=-1, keepdims=True)
        h_ref[...] = (x * lax.rsqrt(ms + RMS_EPS) * g_ref[...]).astype(BF16)

    h = h_ref[...]
    a = _dot(h, wg_ref[...])
    u = _dot(h, wu_ref[...])
    o_ref[...] = (a / (1.0 + jnp.exp(-a)) * u).astype(o_ref.dtype)


def _ffn_up(x2, g, w_gate, w_up, tm, tn):
    m, d = x2.shape
    n = w_gate.shape[1]
    return pl.pallas_call(
        _ffn_up_kernel,
        grid=(m // tm, n // tn),
        in_specs=[
            pl.BlockSpec((tm, d), lambda i, j: (i, 0)),
            pl.BlockSpec((1, d), lambda i, j: (0, 0)),
            pl.BlockSpec((d, tn), lambda i, j: (0, j)),
            pl.BlockSpec((d, tn), lambda i, j: (0, j)),
        ],
        out_specs=pl.BlockSpec((tm, tn), lambda i, j: (i, j)),
        out_shape=jax.ShapeDtypeStruct((m, n), BF16),
        scratch_shapes=[pltpu.VMEM((tm, d), BF16)],
        compiler_params=_params("arbitrary", "arbitrary"),
        name="ffn_up",
    )(x2, g, w_gate, w_up)


def _ffn_down_kernel(x_ref, a_ref, w_ref, o_ref):
    o_ref[...] = x_ref[...] + _dot(a_ref[...], w_ref[...])


def _ffn_down(x2, act, w_down, tm, tn):
    m, d = x2.shape
    f = act.shape[1]
    return pl.pallas_call(
        _ffn_down_kernel,
        grid=(m // tm, d // tn),
        in_specs=[
            pl.BlockSpec((tm, tn), lambda i, j: (i, j)),
            pl.BlockSpec((tm, f), lambda i, j: (i, 0)),
            pl.BlockSpec((f, tn), lambda i, j: (0, j)),
        ],
        out_specs=pl.BlockSpec((tm, tn), lambda i, j: (i, j)),
        out_shape=jax.ShapeDtypeStruct((m, d), F32),
        compiler_params=_params("arbitrary", "arbitrary"),
        name="ffn_down",
    )(x2, act, w_down)


def _layer_weights(w_in, fox_forget_bias, fox_q_norm, fox_k_norm, nsa_q_norm, slc_k_norm, win_k_norm):
    kv_w = NSA_KV_HEADS * HEAD_DIM
    splits = (FOX_W, FOX_W, FOX_W, N_FOX_HEADS, NSA_W, kv_w, kv_w, kv_w, kv_w, kv_w, kv_w, 3 * N_NSA_HEADS)
    cuts = [0]
    for s in splits:
        cuts.append(cuts[-1] + s)
    part = lambda k: w_in[:, cuts[k]:cuts[k + 1]]
    w_all = jnp.concatenate([part(k) for k in (1, 7, 9, 5, 6, 0, 4, 2, 8, 10)], axis=1).astype(BF16)
    d = w_in.shape[0]
    w_aux = jnp.concatenate(
        [part(3), part(11), jnp.zeros((d, LANES - N_FOX_HEADS - 3 * N_NSA_HEADS), w_in.dtype)], axis=1).astype(BF16)

    scale = HEAD_DIM ** -0.5
    ones = lambda n: jnp.ones((n * HEAD_DIM,), F32)
    rep = lambda gvec, n, s=1.0: jnp.tile(gvec.astype(F32) * s, n)
    n_v = N_FOX_HEADS + 2 * NSA_KV_HEADS
    gain = jnp.concatenate([
        rep(fox_k_norm, N_FOX_HEADS), rep(slc_k_norm, NSA_KV_HEADS), rep(win_k_norm, NSA_KV_HEADS),
        ones(CV_HEADS),
        rep(fox_q_norm, N_FOX_HEADS, scale * LOG2E), rep(nsa_q_norm, N_NSA_HEADS, scale), ones(n_v)])[None, :]
    z = lambda n: jnp.zeros((n * HEAD_DIM,), F32)
    flag = jnp.concatenate([
        ones(ROW_HEADS), z(CV_HEADS), ones(N_FOX_HEADS + N_NSA_HEADS), z(n_v)])[None, :]
    bias = jnp.concatenate([fox_forget_bias.astype(F32), jnp.zeros((LANES - N_FOX_HEADS,), F32)])[None, :]
    return w_all, w_aux, gain, flag, bias


def _pick(n, prefs):
    for p in prefs:
        if n % p == 0:
            return p
    return n


def kernel(x, attn_norm, w_in, fox_forget_bias, fox_q_norm, fox_k_norm, nsa_q_norm, cmp_k_norm, slc_k_norm,
           win_k_norm, cmp_pos_k, cmp_pos_v, cmp_w_k, cmp_w_v, w_out, ffn_norm, w_gate, w_up, w_down):
    b, t, d = x.shape
    depth = w_in.shape[0]
    m = b * t
    tm = _pick(m, (ROW_TILE, 512, 256, 128))
    nch = t // CMP_STRIDE
    slopes = jnp.asarray([2.0 ** (-8.0 * (i + 1) / N_NSA_HEADS) for i in range(N_NSA_HEADS)], F32)

    key = jnp.arange(t)[:, None]
    col = jnp.arange(LANES)[None, :]
    pos_cols = jnp.where(col == 0, key % SLC_BLOCK, key // SLC_BLOCK == col).astype(BF16)

    x2 = x.reshape(m, d)
    for l in range(depth):
        w_all, w_aux, gain, flag, bias = _layer_weights(
            w_in[l], fox_forget_bias[l], fox_q_norm[l], fox_k_norm[l], nsa_q_norm[l], slc_k_norm[l], win_k_norm[l])
        proj, cv, vt, aux = _inproj(x2, attn_norm[l][None, :], w_all, w_aux, gain, flag, tm, PROJ_COLS)
        proj = proj.reshape(b, t, ROW_HEADS * HEAD_DIM)
        cum, gates = _gates(aux.reshape(b, t, LANES), bias, _pick(t, (GATE_ROWS, 256, 128)))

        pos = lambda p: jnp.broadcast_to(p.reshape(1, CMP_BLOCK * HEAD_DIM), (8, CMP_BLOCK * HEAD_DIM))
        k_cmp, v_cmp_t = _compress(
            cv.reshape(CV_HEADS, b, nch, CMP_STRIDE * HEAD_DIM),
            cmp_w_k[l].reshape(CMP_BLOCK * HEAD_DIM, HEAD_DIM), cmp_w_v[l].reshape(CMP_BLOCK * HEAD_DIM, HEAD_DIM),
            pos(cmp_pos_k[l]), pos(cmp_pos_v[l]), cmp_k_norm[l][None, :].astype(F32))

        o_fox = _fox(proj, cum, vt, _pick(t, (FOX_TQ, 512)), FOX_TK)
        o_nsa = _nsa(slopes, proj, k_cmp, v_cmp_t, pos_cols, vt, gates, NSA_TQ)

        wo = w_out[l].astype(BF16)
        x2 = _outproj(x2, o_fox.reshape(m, FOX_W), o_nsa.reshape(m, NSA_W), wo[:FOX_W], wo[FOX_W:], tm, OUT_COLS)

        act = _ffn_up(x2, ffn_norm[l][None, :], w_gate[l].astype(BF16), w_up[l].astype(BF16), tm, FFN_COLS)
        x2 = _ffn_down(x2, act, w_down[l].astype(BF16), _pick(m, (DOWN_ROW_TILE, 256, 128)), OUT_COLS)
    return x2.reshape(b, t, d)
```
